```python
import math
import jax, jax.numpy as jnp
from jax import lax
import numpy as np

D_MODEL = 1024
BATCH = 2
SEQ = 8192
DEPTH = 4
DEC_BATCH = 32
DEC_SEQ = 2048
PAST_LEN = 128

GRID_W = 64
Q_BLOCK = 128
ROPE_THETA = 10000.0
RMS_EPS = 1e-6
LN_EPS = 1e-5
NEG_INF = -1e30

MLA_HEADS = 4
MLA_Q_RANK = 384
MLA_KV_RANK = 256
MLA_NOPE = 128
MLA_ROPE = 64
MLA_V = 128

WIN_HEADS = 8
WIN_KV_HEADS = 2
WIN_HEAD_DIM = 64
WINDOW = 128

AX_HEADS = 8
AX_KV_HEADS = 2
AX_HEAD_DIM = 64

N_BRANCH = 3
D_FF = 2816
N_EXPERTS = 8
TOP_K = 2
D_FF_EXPERT = 3584
MOE_BLOCK = 256

DEEPNORM_ALPHA = (2 * DEPTH) ** 0.25
DEEPNORM_BETA = (8 * DEPTH) ** -0.25
N_DENSE = (DEPTH + 1) // 2
N_MOE = DEPTH // 2

MLA_IN = MLA_Q_RANK + MLA_KV_RANK + MLA_ROPE
WIN_IN = (WIN_HEADS + 2 * WIN_KV_HEADS) * WIN_HEAD_DIM
AX_IN = (AX_HEADS + 2 * AX_KV_HEADS) * AX_HEAD_DIM
GATE_IN = N_BRANCH * D_MODEL
D_IN = MLA_IN + WIN_IN + AX_IN + GATE_IN

kernel_name = 'hybrid_gated_mla_window_axial_encoder'


def layer_norm(x, g, b):
    xf = x.astype(jnp.float32)
    mu = jnp.mean(xf, axis=-1, keepdims=True)
    var = jnp.mean(jnp.square(xf - mu), axis=-1, keepdims=True)
    return ((xf - mu) * lax.rsqrt(var + LN_EPS) * g.astype(jnp.float32) + b.astype(jnp.float32)).astype(x.dtype)


def rms_norm(x, g):
    xf = x.astype(jnp.float32)
    ms = jnp.mean(jnp.square(xf), axis=-1, keepdims=True)
    return (xf * lax.rsqrt(ms + RMS_EPS) * g.astype(jnp.float32)).astype(x.dtype)


def rope_freqs(pos, dim):
    inv = ROPE_THETA ** (-jnp.arange(0, dim, 2, dtype=jnp.float32) / dim)
    ang = pos.astype(jnp.float32)[:, None] * inv[None, :]
    return jnp.cos(ang), jnp.sin(ang)


def apply_rope(x, cos, sin):
    x1, x2 = jnp.split(x.astype(jnp.float32), 2, axis=-1)
    c = cos[None, :, None, :]
    s = sin[None, :, None, :]
    return jnp.concatenate([x1 * c - x2 * s, x1 * s + x2 * c], axis=-1).astype(x.dtype)


def apply_axial_rope(x, rope_row, rope_col):
    x_row, x_col = jnp.split(x, 2, axis=-1)
    return jnp.concatenate([apply_rope(x_row, *rope_row), apply_rope(x_col, *rope_col)], axis=-1)


def dense_attention_blocks(q, k, v, scale):
    B, T, H, dq = q.shape
    KV = k.shape[2]
    G = H // KV
    nb = T // Q_BLOCK
    qb = q.reshape(B, nb, Q_BLOCK, KV, G, dq).transpose(1, 0, 2, 3, 4, 5)

    def one_block(qi):
        s = jnp.einsum('bqkgd,bskd->bkgqs', qi, k, preferred_element_type=jnp.float32) * scale
        p = jax.nn.softmax(s, axis=-1)
        return jnp.einsum('bkgqs,bskd->bqkgd', p.astype(v.dtype), v)

    o = lax.map(one_block, qb)
    return o.transpose(1, 0, 2, 3, 4, 5).reshape(B, T, H, v.shape[-1])


def window_attention(q, k, v, sink, scale):
    B, T, H, d = q.shape
    KV = k.shape[2]
    G = H // KV
    nb = T // Q_BLOCK
    span = Q_BLOCK + 2 * WINDOW
    kp = jnp.pad(k, ((0, 0), (WINDOW, WINDOW), (0, 0), (0, 0)))
    vp = jnp.pad(v, ((0, 0), (WINDOW, WINDOW), (0, 0), (0, 0)))
    qb = q.reshape(B, nb, Q_BLOCK, KV, G, d).transpose(1, 0, 2, 3, 4, 5)
    r = jnp.arange(Q_BLOCK)
    j = jnp.arange(span)
    rel = (j[None, :] - WINDOW) - r[:, None]
    dist = jnp.abs(rel).astype(jnp.float32)
    in_band = jnp.abs(rel) <= WINDOW
    slopes = jnp.exp2(-8.0 * jnp.arange(1, H + 1, dtype=jnp.float32) / H).reshape(KV, G)
    alibi = -slopes[:, :, None, None] * dist[None, None]
    sink_f = sink.astype(jnp.float32).reshape(KV, G)

    def one_block(args):
        qi, i = args
        start = i * Q_BLOCK
        ki = lax.dynamic_slice_in_dim(kp, start, span, axis=1)
        vi = lax.dynamic_slice_in_dim(vp, start, span, axis=1)
        s_abs = start - WINDOW + j
        valid = in_band & ((s_abs >= 0) & (s_abs < T))[None, :]
        logits = jnp.einsum('bqkgd,bskd->bkgqs', qi, ki, preferred_element_type=jnp.float32) * scale
        logits = jnp.where(valid, logits + alibi[None], NEG_INF)
        sink_l = jnp.broadcast_to(sink_f[None, :, :, None, None], logits.shape[:-1] + (1,))
        p = jax.nn.softmax(jnp.concatenate([logits, sink_l], axis=-1), axis=-1)[..., :-1]
        return jnp.einsum('bkgqs,bskd->bqkgd', p.astype(vi.dtype), vi)

    o = lax.map(one_block, (qb, jnp.arange(nb)))
    return o.transpose(1, 0, 2, 3, 4, 5).reshape(B, T, H, d)


def split_columns(proj):
    sizes = [MLA_Q_RANK, MLA_KV_RANK, MLA_ROPE,
             WIN_HEADS * WIN_HEAD_DIM, WIN_KV_HEADS * WIN_HEAD_DIM, WIN_KV_HEADS * WIN_HEAD_DIM,
             AX_HEADS * AX_HEAD_DIM, AX_KV_HEADS * AX_HEAD_DIM, AX_KV_HEADS * AX_HEAD_DIM,
             GATE_IN]
    offsets = []
    acc = 0
    for s in sizes[:-1]:
        acc += s
        offsets.append(acc)
    return jnp.split(proj, offsets, axis=-1)


def token_mixer(x, rope_t, rope_row, rope_col, w_in, q_norm_g, kv_norm_g, w_uq, w_ukv, sink,
                axq_g, axk_g, w_br_mla, w_br_win, w_br_ax, w_out):
    B, T, _ = x.shape
    proj = jnp.einsum('btd,de->bte', x, w_in)
    c_q, c_kv, k_r, wq, wk, wv, aq, ak, av, g = split_columns(proj)

    c_q = rms_norm(c_q, q_norm_g)
    q = jnp.einsum('btr,re->bte', c_q, w_uq).reshape(B, T, MLA_HEADS, MLA_NOPE + MLA_ROPE)
    q = jnp.concatenate([q[..., :MLA_NOPE], apply_rope(q[..., MLA_NOPE:], *rope_t)], axis=-1)
    c_kv = rms_norm(c_kv, kv_norm_g)
    kv = jnp.einsum('btr,re->bte', c_kv, w_ukv).reshape(B, T, MLA_HEADS, MLA_NOPE + MLA_V)
    k_rope = jnp.broadcast_to(apply_rope(k_r[:, :, None, :], *rope_t), (B, T, MLA_HEADS, MLA_ROPE))
    k = jnp.concatenate([kv[..., :MLA_NOPE], k_rope], axis=-1)
    v = kv[..., MLA_NOPE:]
    o_a = dense_attention_blocks(q, k, v, (MLA_NOPE + MLA_ROPE) ** -0.5).reshape(B, T, MLA_HEADS * MLA_V)

    o_b = window_attention(wq.reshape(B, T, WIN_HEADS, WIN_HEAD_DIM),
                           wk.reshape(B, T, WIN_KV_HEADS, WIN_HEAD_DIM),
                           wv.reshape(B, T, WIN_KV_HEADS, WIN_HEAD_DIM),
                           sink, WIN_HEAD_DIM ** -0.5).reshape(B, T, WIN_HEADS * WIN_HEAD_DIM)

    aq = apply_axial_rope(rms_norm(aq.reshape(B, T, AX_HEADS, AX_HEAD_DIM), axq_g), rope_row, rope_col)
    ak = apply_axial_rope(rms_norm(ak.reshape(B, T, AX_KV_HEADS, AX_HEAD_DIM), axk_g), rope_row, rope_col)
    o_c = dense_attention_blocks(aq, ak, av.reshape(B, T, AX_KV_HEADS, AX_HEAD_DIM),
                                 AX_HEAD_DIM ** -0.5).reshape(B, T, AX_HEADS * AX_HEAD_DIM)

    gates = jax.nn.sigmoid(g.astype(jnp.float32)).astype(x.dtype).reshape(B, T, N_BRANCH, D_MODEL)
    merged = (gates[:, :, 0] * jnp.einsum('bte,ed->btd', o_a, w_br_mla)
              + gates[:, :, 1] * jnp.einsum('bte,ed->btd', o_b, w_br_win)
              + gates[:, :, 2] * jnp.einsum('bte,ed->btd', o_c, w_br_ax))
    return jnp.einsum('btd,de->bte', merged, w_out)


def swiglu(x, w1, w3, w2):
    h = jax.nn.silu(jnp.einsum('btd,df->btf', x, w1)) * jnp.einsum('btd,df->btf', x, w3)
    return jnp.einsum('btf,fd->btd', h, w2)


def moe_swiglu(x, w_router, w1, w3, w2):
    B, T, D = x.shape
    xt = x.reshape(-1, D)
    N = xt.shape[0]
    A = N * TOP_K
    logits = jnp.einsum('nd,de->ne', xt, w_router, preferred_element_type=jnp.float32)
    top_l, top_e = lax.top_k(logits, TOP_K)
    gate = jax.nn.softmax(top_l, axis=-1)
    e_flat = top_e.reshape(-1)
    tok_flat = jnp.repeat(jnp.arange(N), TOP_K)
    g_flat = gate.reshape(-1)
    order = jnp.argsort(e_flat)
    e_s = e_flat[order]
    tok_s = tok_flat[order]
    g_s = g_flat[order]
    counts = jnp.bincount(e_flat, length=N_EXPERTS)
    padded = (counts + MOE_BLOCK - 1) // MOE_BLOCK * MOE_BLOCK
    pad_end = jnp.cumsum(padded)
    pad_start = pad_end - padded
    start = jnp.cumsum(counts) - counts
    dest = pad_start[e_s] + jnp.arange(A) - start[e_s]
    n_blocks = -(-A // MOE_BLOCK) + N_EXPERTS
    P = n_blocks * MOE_BLOCK
    buf = jnp.zeros((P, D), x.dtype).at[dest].set(xt[tok_s])
    blk_expert = jnp.minimum(jnp.searchsorted(pad_end, jnp.arange(n_blocks) * MOE_BLOCK, side='right'),
                             N_EXPERTS - 1)

    def expert_block(args):
        xb, e = args
        h = jax.nn.silu(xb @ w1[e]) * (xb @ w3[e])
        return h @ w2[e]

    yb = lax.map(expert_block, (buf.reshape(n_blocks, MOE_BLOCK, D), blk_expert))
    y_s = yb.reshape(P, D)[dest] * g_s[:, None].astype(x.dtype)
    y = jnp.zeros((N, D), x.dtype).at[tok_s].add(y_s)
    return y.reshape(B, T, D)


def trunk(x, emb_ln_g, emb_ln_b, w_in, mla_q_norm, mla_kv_norm, w_uq, w_ukv, win_sink,
          ax_q_norm, ax_k_norm, w_br_mla, w_br_win, w_br_ax, w_out, ln1_g, ln1_b, ln2_g, ln2_b,
          ffn_w1, ffn_w3, ffn_w2, moe_router, moe_w1, moe_w3, moe_w2):
    T = x.shape[1]
    rows = T // GRID_W
    t = jnp.arange(T)
    row = jnp.repeat(jnp.arange(rows), GRID_W)
    col = jnp.tile(jnp.arange(GRID_W), rows)
    rope_t = rope_freqs(t, MLA_ROPE)
    rope_row = rope_freqs(row, AX_HEAD_DIM // 2)
    rope_col = rope_freqs(col, AX_HEAD_DIM // 2)

    x = layer_norm(x, emb_ln_g, emb_ln_b)
    for l in range(DEPTH):
        m = token_mixer(x, rope_t, rope_row, rope_col, w_in[l], mla_q_norm[l], mla_kv_norm[l],
                        w_uq[l], w_ukv[l], win_sink[l], ax_q_norm[l], ax_k_norm[l],
                        w_br_mla[l], w_br_win[l], w_br_ax[l], w_out[l])
        x = layer_norm(DEEPNORM_ALPHA * x + m, ln1_g[l], ln1_b[l])
        if l % 2 == 0:
            i = l // 2
            f = swiglu(x, ffn_w1[i], ffn_w3[i], ffn_w2[i])
        else:
            i = l // 2
            f = moe_swiglu(x, moe_router[i], moe_w1[i], moe_w3[i], moe_w2[i])
        x = layer_norm(DEEPNORM_ALPHA * x + f, ln2_g[l], ln2_b[l])
    return x


def setup_inputs(seed: int = 0) -> dict:
    key = jax.random.key(seed)
    ks = jax.random.split(key, 32)
    nrm = lambda k, shape, scale: jax.random.normal(k, shape, jnp.float32) * scale
    gain = lambda k, shape: 1.0 + 0.05 * jax.random.normal(k, shape, jnp.float32)
    br_in = MLA_HEADS * MLA_V
    return {
        'x_prompt': nrm(ks[0], (BATCH, SEQ, D_MODEL), 1.0),
        'x_sample': nrm(ks[1], (DEC_BATCH, DEC_SEQ, D_MODEL), 1.0),
        'emb_ln_g': gain(ks[2], (D_MODEL,)),
        'emb_ln_b': nrm(ks[3], (D_MODEL,), 0.02),
        'w_in': nrm(ks[4], (DEPTH, D_MODEL, D_IN), D_MODEL ** -0.5),
        'mla_q_norm': gain(ks[5], (DEPTH, MLA_Q_RANK)),
        'mla_kv_norm': gain(ks[6], (DEPTH, MLA_KV_RANK)),
        'w_uq': nrm(ks[7], (DEPTH, MLA_Q_RANK, MLA_HEADS * (MLA_NOPE + MLA_ROPE)), MLA_Q_RANK ** -0.5),
        'w_ukv': nrm(ks[8], (DEPTH, MLA_KV_RANK, MLA_HEADS * (MLA_NOPE + MLA_V)), MLA_KV_RANK ** -0.5),
        'win_sink': nrm(ks[9], (DEPTH, WIN_HEADS), 0.5),
        'ax_q_norm': gain(ks[10], (DEPTH, AX_HEAD_DIM)),
        'ax_k_norm': gain(ks[11], (DEPTH, AX_HEAD_DIM)),
        'w_br_mla': nrm(ks[12], (DEPTH, br_in, D_MODEL), br_in ** -0.5),
        'w_br_win': nrm(ks[13], (DEPTH, WIN_HEADS * WIN_HEAD_DIM, D_MODEL), (WIN_HEADS * WIN_HEAD_DIM) ** -0.5),
        'w_br_ax': nrm(ks[14], (DEPTH, AX_HEADS * AX_HEAD_DIM, D_MODEL), (AX_HEADS * AX_HEAD_DIM) ** -0.5),
        'w_out': nrm(ks[15], (DEPTH, D_MODEL, D_MODEL), DEEPNORM_BETA * D_MODEL ** -0.5),
        'ln1_g': gain(ks[16], (DEPTH, D_MODEL)),
        'ln1_b': nrm(ks[17], (DEPTH, D_MODEL), 0.02),
        'ln2_g': gain(ks[18], (DEPTH, D_MODEL)),
        'ln2_b': nrm(ks[19], (DEPTH, D_MODEL), 0.02),
        'ffn_w1': nrm(ks[20], (N_DENSE, D_MODEL, D_FF), D_MODEL ** -0.5),
        'ffn_w3': nrm(ks[21], (N_DENSE, D_MODEL, D_FF), D_MODEL ** -0.5),
        'ffn_w2': nrm(ks[22], (N_DENSE, D_FF, D_MODEL), DEEPNORM_BETA * D_FF ** -0.5),
        'moe_router': nrm(ks[23], (N_MOE, D_MODEL, N_EXPERTS), D_MODEL ** -0.5),
        'moe_w1': nrm(ks[24], (N_MOE, N_EXPERTS, D_MODEL, D_FF_EXPERT), D_MODEL ** -0.5),
        'moe_w3': nrm(ks[25], (N_MOE, N_EXPERTS, D_MODEL, D_FF_EXPERT), D_MODEL ** -0.5),
        'moe_w2': nrm(ks[26], (N_MOE, N_EXPERTS, D_FF_EXPERT, D_MODEL), DEEPNORM_BETA * D_FF_EXPERT ** -0.5),
    }


def reference(x_prompt, x_sample, emb_ln_g, emb_ln_b, w_in, mla_q_norm, mla_kv_norm, w_uq, w_ukv,
              win_sink, ax_q_norm, ax_k_norm, w_br_mla, w_br_win, w_br_ax, w_out, ln1_g, ln1_b,
              ln2_g, ln2_b, ffn_w1, ffn_w3, ffn_w2, moe_router, moe_w1, moe_w3, moe_w2):
    y_prompt = trunk(x_prompt, emb_ln_g, emb_ln_b, w_in, mla_q_norm, mla_kv_norm, w_uq, w_ukv, win_sink,
                     ax_q_norm, ax_k_norm, w_br_mla, w_br_win, w_br_ax, w_out, ln1_g, ln1_b, ln2_g, ln2_b,
                     ffn_w1, ffn_w3, ffn_w2, moe_router, moe_w1, moe_w3, moe_w2)
    y_sample = trunk(x_sample, emb_ln_g, emb_ln_b, w_in, mla_q_norm, mla_kv_norm, w_uq, w_ukv, win_sink,
                     ax_q_norm, ax_k_norm, w_br_mla, w_br_win, w_br_ax, w_out, ln1_g, ln1_b, ln2_g, ln2_b,
                     ffn_w1, ffn_w3, ffn_w2, moe_router, moe_w1, moe_w3, moe_w2)
    return (y_prompt, y_sample)
```

```python
import functools
import math

import jax
import jax.numpy as jnp
from jax import lax
from jax.experimental import pallas as pl
from jax.experimental.pallas import tpu as pltpu

F32 = jnp.float32
BF16 = jnp.bfloat16

D_MODEL = 1024
GRID_W = 64
ROPE_THETA = 10000.0
RMS_EPS = 1e-6
LN_EPS = 1e-5
NEG_INF = -1e30

MLA_HEADS = 4
MLA_Q_RANK = 384
MLA_KV_RANK = 256
MLA_NOPE = 128
MLA_ROPE = 64
MLA_V = 128
MLA_QK = MLA_NOPE + MLA_ROPE

HEADS = 8
KV_HEADS = 2
HEAD_DIM = 64
GROUP = HEADS // KV_HEADS
WINDOW = 128
Q_BLOCK = 128
SPAN = Q_BLOCK + 2 * WINDOW

N_BRANCH = 3
N_EXPERTS = 8
TOP_K = 2

LANES = 128
MXU_DIM = 256
VMEM_LIMIT = 56 * 1024 * 1024

MLA_QK_PAD = MXU_DIM
ATT_V_PAD = MXU_DIM
QW = HEADS * HEAD_DIM
KW = KV_HEADS * HEAD_DIM
C_CQ = 0
C_CKV = C_CQ + MLA_Q_RANK
C_KR = C_CKV + MLA_KV_RANK
C_WQ = C_KR + LANES
C_WK = C_WQ + QW
C_WV = C_WK + KW
C_AQ = C_WV + KW
C_AK = C_AQ + QW
C_AV = C_AK + KW
C_END = C_AV + KW
TAB_W = 6 * LANES

HEAD_PERM = [h for c in range(GROUP) for h in (c, GROUP + c)]


def _cparams(sem, vmem=VMEM_LIMIT):
    return pltpu.CompilerParams(dimension_semantics=sem, vmem_limit_bytes=vmem)


def _pick_tile(n, pref):
    t = min(pref, n)
    while n % t:
        t //= 2
    return t


def _ln_rows(y, g, b):
    mu = jnp.mean(y, axis=-1, keepdims=True)
    d = y - mu
    var = jnp.mean(d * d, axis=-1, keepdims=True)
    return d * lax.rsqrt(var + LN_EPS) * g + b


def _rms_rows(x, g):
    ms = jnp.mean(x * x, axis=-1, keepdims=True)
    return x * lax.rsqrt(ms + RMS_EPS) * g


def _rope_chunk(x, cos, sin_a, sin_b, half):
    return (x * cos + pltpu.roll(x, LANES - half, 1) * sin_a + pltpu.roll(x, half, 1) * sin_b)


def _dot(a, b):
    return jnp.dot(a, b, preferred_element_type=F32)


def _dot_nt(a, b):
    return lax.dot_general(a, b, (((1,), (1,)), ((), ())), preferred_element_type=F32)


def _split_dot(x, w_bf16):
    hi = x.astype(BF16)
    lo = (x - hi.astype(F32)).astype(BF16)
    return _dot(hi, w_bf16) + _dot(lo, w_bf16)


def _ln_kernel(x_ref, g_ref, b_ref, o_ref):
    o_ref[...] = _ln_rows(x_ref[...], g_ref[...], b_ref[...])


def _embed_ln(x, g, b, tm):
    n = x.shape[0]
    return pl.pallas_call(
        _ln_kernel,
        grid=(n // tm,),
        in_specs=[pl.BlockSpec((tm, D_MODEL), lambda i: (i, 0)),
                  pl.BlockSpec((1, D_MODEL), lambda i: (0, 0)),
                  pl.BlockSpec((1, D_MODEL), lambda i: (0, 0))],
        out_specs=pl.BlockSpec((tm, D_MODEL), lambda i: (i, 0)),
        out_shape=jax.ShapeDtypeStruct((n, D_MODEL), F32),
        compiler_params=_cparams(("parallel",)),
        name="embed_ln",
    )(x, g.reshape(1, -1), b.reshape(1, -1))


def _proj_kernel(pos_ref, x_ref, ws_ref, wuq_ref, wukv_ref, gq_ref, gkv_ref, gaq_ref, gak_ref,
                 bd_ref, tab_ref,
                 qm_ref, km_ref, vm_ref, wq_ref, wk_ref, wv_ref, aq_ref, ak_ref, av_ref):
    del pos_ref
    tm = x_ref.shape[0]
    x = x_ref[...].astype(BF16)
    res = _dot(x, ws_ref[...])

    m_cos = tab_ref[:, 0 * LANES:1 * LANES]
    m_sa = tab_ref[:, 1 * LANES:2 * LANES]
    m_sb = tab_ref[:, 2 * LANES:3 * LANES]
    a_cos = tab_ref[:, 3 * LANES:4 * LANES]
    a_sa = tab_ref[:, 4 * LANES:5 * LANES]
    a_sb = tab_ref[:, 5 * LANES:6 * LANES]
    ones_col = jnp.where(lax.broadcasted_iota(jnp.int32, (tm, LANES), 1) == 0, 1.0, 0.0).astype(BF16)

    c_q = _rms_rows(res[:, C_CQ:C_CKV], gq_ref[...]).astype(BF16)
    q = _dot(c_q, wuq_ref[...])
    q_scale = MLA_QK ** -0.5
    for h in range(MLA_HEADS):
        base = h * MLA_QK_PAD
        qm_ref[:, base:base + LANES] = (q[:, base:base + LANES] * q_scale).astype(BF16)
        roped = _rope_chunk(q[:, base + LANES:base + 2 * LANES], m_cos, m_sa, m_sb, MLA_ROPE // 2)
        qm_ref[:, base + LANES:base + 2 * LANES] = (roped * q_scale).astype(BF16)

    c_kv = _rms_rows(res[:, C_CKV:C_KR], gkv_ref[...]).astype(BF16)
    kv = _dot(c_kv, wukv_ref[...])
    k_rope = _rope_chunk(res[:, C_KR:C_WQ], m_cos, m_sa, m_sb, MLA_ROPE // 2).astype(BF16)
    for h in range(MLA_HEADS):
        base = h * MLA_QK_PAD
        km_ref[:, base:base + LANES] = kv[:, h * LANES:(h + 1) * LANES].astype(BF16)
        km_ref[:, base + LANES:base + 2 * LANES] = k_rope
        vbase = h * ATT_V_PAD
        v_off = MLA_HEADS * MLA_NOPE
        vm_ref[:, vbase:vbase + LANES] = kv[:, v_off + h * LANES:v_off + (h + 1) * LANES].astype(BF16)
        vm_ref[:, vbase + LANES:vbase + 2 * LANES] = ones_col

    wq_ref[...] = (res[:, C_WQ:C_WK] * (HEAD_DIM ** -0.5)).astype(BF16)
    wk_ref[...] = res[:, C_WK:C_WV].astype(BF16)
    wv_ref[:, 0:LANES] = res[:, C_WV:C_AQ].astype(BF16)
    wv_ref[:, LANES:2 * LANES] = ones_col

    bd = bd_ref[...]

    def norm_rope(xc, g):
        ms = _split_dot(xc * xc, bd)
        xn = xc * lax.rsqrt(ms + RMS_EPS) * g
        return _rope_chunk(xn, a_cos, a_sa, a_sb, HEAD_DIM // 4)

    for c in range(QW // LANES):
        xc = res[:, C_AQ + c * LANES:C_AQ + (c + 1) * LANES]
        aq_ref[:, c * LANES:(c + 1) * LANES] = (norm_rope(xc, gaq_ref[...]) * (HEAD_DIM ** -0.5)).astype(BF16)
    ak_ref[...] = norm_rope(res[:, C_AK:C_AV], gak_ref[...]).astype(BF16)
    av_ref[:, 0:LANES] = res[:, C_AV:C_END].astype(BF16)
    av_ref[:, LANES:2 * LANES] = ones_col


def _in_proj(x, pos_blk, lw, tab, tm):
    n = x.shape[0]
    const = lambda shape: pl.BlockSpec(shape, lambda i, p: (0, 0))
    rows = lambda w: pl.BlockSpec((tm, w), lambda i, p: (i, 0))
    out_widths = [MLA_HEADS * MLA_QK_PAD, MLA_HEADS * MLA_QK_PAD, MLA_HEADS * ATT_V_PAD,
                  QW, KW, ATT_V_PAD, QW, KW, ATT_V_PAD]
    grid_spec = pltpu.PrefetchScalarGridSpec(
        num_scalar_prefetch=1,
        grid=(n // tm,),
        in_specs=[rows(D_MODEL),
                  const((D_MODEL, C_END)),
                  const((MLA_Q_RANK, MLA_HEADS * MLA_QK_PAD)),
                  const((MLA_KV_RANK, 2 * MLA_HEADS * LANES)),
                  const((1, MLA_Q_RANK)), const((1, MLA_KV_RANK)),
                  const((1, LANES)), const((1, LANES)),
                  const((LANES, LANES)),
                  pl.BlockSpec((tm, TAB_W), lambda i, p: (p[i], 0))],
        out_specs=[rows(w) for w in out_widths],
    )
    return pl.pallas_call(
        _proj_kernel,
        grid_spec=grid_spec,
        out_shape=[jax.ShapeDtypeStruct((n, w), BF16) for w in out_widths],
        compiler_params=_cparams(("parallel",)),
        name="in_proj",
    )(pos_blk, x, lw["w_s"], lw["w_uq"], lw["w_ukv"], lw["g_q"], lw["g_kv"], lw["g_aq"], lw["g_ak"],
      lw["bd"], tab)


def _flash(q, k_ref, v_ref, m_sc, acc_sc, seq, tk):
    m_sc[...] = jnp.full(m_sc.shape, NEG_INF, F32)
    acc_sc[...] = jnp.zeros(acc_sc.shape, F32)

    def step(j, carry):
        start = pl.multiple_of(j * tk, tk)
        s = _dot_nt(q, k_ref[pl.ds(start, tk), :])
        m_prev = m_sc[:, 0:1]
        m_new = jnp.maximum(m_prev, jnp.max(s, axis=1, keepdims=True))
        alpha = jnp.exp(m_prev - m_new)
        p = jnp.exp(s - m_new)
        pv = _dot(p.astype(BF16), v_ref[pl.ds(start, tk), :])
        acc_sc[...] = acc_sc[...] * alpha + pv
        m_sc[...] = jnp.broadcast_to(m_new, m_sc.shape)
        return carry

    lax.fori_loop(0, seq // tk, step, 0)
    return acc_sc[...]


def _stack_heads(q):
    rows = q.shape[0]
    lo = lax.broadcasted_iota(jnp.int32, (rows, LANES), 1) < HEAD_DIM
    zero = jnp.zeros((rows, LANES), q.dtype)
    chunks = [q[:, c * LANES:(c + 1) * LANES] for c in range(GROUP)]
    parts = [jnp.where(lo, ch, zero) for ch in chunks] + [jnp.where(lo, zero, ch) for ch in chunks]
    return jnp.concatenate(parts, axis=0)


def _unstack_heads(o, rows, o_ref):
    lo = lax.broadcasted_iota(jnp.int32, (rows, LANES), 1) < HEAD_DIM
    for c in range(GROUP):
        top = o[c * rows:(c + 1) * rows]
        bot = o[(GROUP + c) * rows:(GROUP + c + 1) * rows]
        o_ref[:, c * LANES:(c + 1) * LANES] = jnp.where(lo, top, bot).astype(o_ref.dtype)


def _gqa_kernel(q_ref, k_ref, v_ref, o_ref, m_sc, acc_sc, *, seq, tk):
    tq = q_ref.shape[0]
    acc = _flash(_stack_heads(q_ref[...]), k_ref, v_ref, m_sc, acc_sc, seq, tk)
    o = acc[:, 0:LANES] / acc[:, LANES:LANES + 1]
    _unstack_heads(o, tq, o_ref)


def _axial_attention(q, k, v, tok_off, batch, seq):
    tq = _pick_tile(seq, 128)
    tk = _pick_tile(seq, 512)
    nq = seq // tq
    qb, sb = tok_off // tq, tok_off // seq
    return pl.pallas_call(
        functools.partial(_gqa_kernel, seq=seq, tk=tk),
        grid=(batch, nq),
        in_specs=[pl.BlockSpec((tq, QW), lambda b, i: (qb + b * nq + i, 0)),
                  pl.BlockSpec((seq, KW), lambda b, i: (sb + b, 0)),
                  pl.BlockSpec((seq, ATT_V_PAD), lambda b, i: (sb + b, 0))],
        out_specs=pl.BlockSpec((tq, QW), lambda b, i: (b * nq + i, 0)),
        out_shape=jax.ShapeDtypeStruct((batch * seq, QW), BF16),
        scratch_shapes=[pltpu.VMEM((HEADS * tq, LANES), F32),
                        pltpu.VMEM((HEADS * tq, ATT_V_PAD), F32)],
        compiler_params=_cparams(("parallel", "arbitrary")),
        name="axial_attention",
    )(q, k, v)


def _mla_kernel(q_ref, k_ref, v_ref, o_ref, m_sc, acc_sc, *, seq, tk):
    acc = _flash(q_ref[...], k_ref, v_ref, m_sc, acc_sc, seq, tk)
    o_ref[...] = (acc[:, 0:LANES] / acc[:, LANES:LANES + 1]).astype(o_ref.dtype)


def _mla_attention(q, k, v, tok_off, batch, seq):
    tq = _pick_tile(seq, 512)
    tk = _pick_tile(seq, 512)
    nq = seq // tq
    qb, sb = tok_off // tq, tok_off // seq
    return pl.pallas_call(
        functools.partial(_mla_kernel, seq=seq, tk=tk),
        grid=(batch, MLA_HEADS, nq),
        in_specs=[pl.BlockSpec((tq, MLA_QK_PAD), lambda b, h, i: (qb + b * nq + i, h)),
                  pl.BlockSpec((seq, MLA_QK_PAD), lambda b, h, i: (sb + b, h)),
                  pl.BlockSpec((seq, ATT_V_PAD), lambda b, h, i: (sb + b, h))],
        out_specs=pl.BlockSpec((tq, MLA_V), lambda b, h, i: (b * nq + i, h)),
        out_shape=jax.ShapeDtypeStruct((batch * seq, MLA_HEADS * MLA_V), BF16),
        scratch_shapes=[pltpu.VMEM((tq, LANES), F32),
                        pltpu.VMEM((tq, ATT_V_PAD), F32)],
        compiler_params=_cparams(("parallel", "parallel", "arbitrary")),
        name="mla_attention",
    )(q, k, v)


def _window_kernel(sink_ref, q_ref, k_ref, v_ref, o_ref, *, seq):
    i = pl.program_id(1)
    start = i * Q_BLOCK
    kstart = pl.multiple_of(jnp.clip(start - WINDOW, 0, seq - SPAN), Q_BLOCK)
    k = k_ref[pl.ds(kstart, SPAN), :]
    v = v_ref[pl.ds(kstart, SPAN), :]
    s_all = _dot_nt(_stack_heads(q_ref[...]), k)
    rel = ((kstart + lax.broadcasted_iota(jnp.int32, (Q_BLOCK, SPAN), 1))
           - (start + lax.broadcasted_iota(jnp.int32, (Q_BLOCK, SPAN), 0)))
    dist_i = jnp.abs(rel)
    dist = dist_i.astype(F32)
    valid = dist_i <= WINDOW
    outs = []
    for h in range(HEADS):
        slope = 2.0 ** (-8.0 * (h + 1) / HEADS)
        sink = sink_ref[h]
        logits = jnp.where(valid, s_all[h * Q_BLOCK:(h + 1) * Q_BLOCK] - slope * dist, NEG_INF)
        m = jnp.maximum(jnp.max(logits, axis=1, keepdims=True), sink)
        p = jnp.exp(logits - m)
        pv = _dot(p.astype(BF16), v)
        denom = pv[:, LANES:LANES + 1] + jnp.exp(sink - m)
        outs.append(pv[:, 0:LANES] / denom)
    _unstack_heads(jnp.concatenate(outs, axis=0), Q_BLOCK, o_ref)


def _window_attention(q, k, v, sink, tok_off, batch, seq):
    assert seq >= SPAN and seq % Q_BLOCK == 0
    nq = seq // Q_BLOCK
    qb, sb = tok_off // Q_BLOCK, tok_off // seq
    return pl.pallas_call(
        functools.partial(_window_kernel, seq=seq),
        grid=(batch, nq),
        in_specs=[pl.BlockSpec(memory_space=pltpu.SMEM),
                  pl.BlockSpec((Q_BLOCK, QW), lambda b, i: (qb + b * nq + i, 0)),
                  pl.BlockSpec((seq, KW), lambda b, i: (sb + b, 0)),
                  pl.BlockSpec((seq, ATT_V_PAD), lambda b, i: (sb + b, 0))],
        out_specs=pl.BlockSpec((Q_BLOCK, QW), lambda b, i: (b * nq + i, 0)),
        out_shape=jax.ShapeDtypeStruct((batch * seq, QW), BF16),
        compiler_params=_cparams(("parallel", "arbitrary")),
        name="window_attention",
    )(sink, q, k, v)


def _merge_kernel(x_ref, oa_ref, ob_ref, oc_ref, wg_ref, wa_ref, wb_ref, wc_ref, wo_ref, g_ref, b_ref,
                  o_ref, *, alpha):
    x = x_ref[...]
    xb = x.astype(BF16)
    merged = None
    for idx, (o_br, w_br) in enumerate(((oa_ref, wa_ref), (ob_ref, wb_ref), (oc_ref, wc_ref))):
        gate = jax.nn.sigmoid(_dot(xb, wg_ref[:, idx * D_MODEL:(idx + 1) * D_MODEL]))
        term = gate * _dot(o_br[...], w_br[...])
        merged = term if merged is None else merged + term
    m = _dot(merged.astype(BF16), wo_ref[...])
    o_ref[...] = _ln_rows(alpha * x + m, g_ref[...], b_ref[...])


def _merge(x, o_a, o_b, o_c, lw, alpha, tm):
    n = x.shape[0]
    const = lambda shape: pl.BlockSpec(shape, lambda i: (0, 0))
    rows = lambda w: pl.BlockSpec((tm, w), lambda i: (i, 0))
    return pl.pallas_call(
        functools.partial(_merge_kernel, alpha=alpha),
        grid=(n // tm,),
        in_specs=[rows(D_MODEL), rows(MLA_HEADS * MLA_V), rows(QW), rows(QW),
                  const((D_MODEL, N_BRANCH * D_MODEL)),
                  const((MLA_HEADS * MLA_V, D_MODEL)), const((QW, D_MODEL)), const((QW, D_MODEL)),
                  const((D_MODEL, D_MODEL)), const((1, D_MODEL)), const((1, D_MODEL))],
        out_specs=rows(D_MODEL),
        out_shape=jax.ShapeDtypeStruct((n, D_MODEL), F32),
        compiler_params=_cparams(("parallel",)),
        name="gated_merge",
    )(x, o_a, o_b, o_c, lw["w_g"], lw["w_br_mla"], lw["w_br_win"], lw["w_br_ax"], lw["w_out"],
      lw["ln1_g"], lw["ln1_b"])


def _ffn_kernel(x_ref, w1_ref, w3_ref, w2_ref, g_ref, b_ref, o_ref, acc_ref, *, alpha):
    f = pl.program_id(1)

    @pl.when(f == 0)
    def _():
        acc_ref[...] = jnp.zeros(acc_ref.shape, F32)

    xb = x_ref[...].astype(BF16)
    h = jax.nn.silu(_dot(xb, w1_ref[...])) * _dot(xb, w3_ref[...])
    acc_ref[...] += _dot(h.astype(BF16), w2_ref[...])

    @pl.when(f == pl.num_programs(1) - 1)
    def _():
        o_ref[...] = _ln_rows(alpha * x_ref[...] + acc_ref[...], g_ref[...], b_ref[...])


def _dense_ffn(x, w1, w3, w2, g, b, alpha, tm, tf):
    n = x.shape[0]
    d_ff = w1.shape[1]
    return pl.pallas_call(
        functools.partial(_ffn_kernel, alpha=alpha),
        grid=(n // tm, d_ff // tf),
        in_specs=[pl.BlockSpec((tm, D_MODEL), lambda i, f: (i, 0)),
                  pl.BlockSpec((D_MODEL, tf), lambda i, f: (0, f)),
                  pl.BlockSpec((D_MODEL, tf), lambda i, f: (0, f)),
                  pl.BlockSpec((tf, D_MODEL), lambda i, f: (f, 0)),
                  pl.BlockSpec((1, D_MODEL), lambda i, f: (0, 0)),
                  pl.BlockSpec((1, D_MODEL), lambda i, f: (0, 0))],
        out_specs=pl.BlockSpec((tm, D_MODEL), lambda i, f: (i, 0)),
        out_shape=jax.ShapeDtypeStruct((n, D_MODEL), F32),
        scratch_shapes=[pltpu.VMEM((tm, D_MODEL), F32)],
        compiler_params=_cparams(("parallel", "arbitrary")),
        name="dense_ffn",
    )(x, w1, w3, w2, g, b)


def _router_kernel(x_ref, whi_ref, wlo_ref, e_ref, g_ref):
    x = x_ref[...]
    hi = x.astype(BF16)
    lo = (x - hi.astype(F32)).astype(BF16)
    logits = _dot(hi, whi_ref[...]) + _dot(lo, whi_ref[...]) + _dot(hi, wlo_ref[...])
    lane = lax.broadcasted_iota(jnp.int32, logits.shape, 1)
    logits = jnp.where(lane < N_EXPERTS, logits, NEG_INF)
    t1 = jnp.max(logits, axis=1, keepdims=True)
    e1 = jnp.min(jnp.where(logits == t1, lane, LANES), axis=1, keepdims=True)
    rest = jnp.where(lane == e1, NEG_INF, logits)
    t2 = jnp.max(rest, axis=1, keepdims=True)
    e2 = jnp.min(jnp.where(rest == t2, lane, LANES), axis=1, keepdims=True)
    w = jnp.exp(t2 - t1)
    g1 = 1.0 / (1.0 + w)
    g2 = w / (1.0 + w)
    e_ref[...] = jnp.where(lane == 0, e1, jnp.where(lane == 1, e2, 0))
    g_ref[...] = jnp.where(lane == 0, g1, jnp.where(lane == 1, g2, 0.0))


def _router(x, w_hi, w_lo, tm):
    n = x.shape[0]
    return pl.pallas_call(
        _router_kernel,
        grid=(n // tm,),
        in_specs=[pl.BlockSpec((tm, D_MODEL), lambda i: (i, 0)),
                  pl.BlockSpec((D_MODEL, LANES), lambda i: (0, 0)),
                  pl.BlockSpec((D_MODEL, LANES), lambda i: (0, 0))],
        out_specs=[pl.BlockSpec((tm, LANES), lambda i: (i, 0)),
                   pl.BlockSpec((tm, LANES), lambda i: (i, 0))],
        out_shape=[jax.ShapeDtypeStruct((n, LANES), jnp.int32),
                   jax.ShapeDtypeStruct((n, LANES), F32)],
        compiler_params=_cparams(("parallel",)),
        name="moe_router",
    )(x, w_hi, w_lo)


GATHER_ROWS = 1024


def _gather_kernel(idx_hbm, x_hbm, o_ref, idx_smem, idx_sem, row_sem):
    i = pl.program_id(0)
    rows = o_ref.shape[0]
    cp = pltpu.make_async_copy(idx_hbm.at[pl.ds(pl.multiple_of(i * rows, rows), rows)], idx_smem, idx_sem)
    cp.start()
    cp.wait()

    def row_copy(r):
        return pltpu.make_async_copy(x_hbm.at[pl.ds(idx_smem[r], 1)], o_ref.at[pl.ds(r, 1)], row_sem)

    def issue(r, carry):
        row_copy(r).start()
        return carry

    def drain(r, carry):
        row_copy(r).wait()
        return carry

    lax.fori_loop(0, rows, issue, 0, unroll=8)
    lax.fori_loop(0, rows, drain, 0, unroll=8)


def _gather_rows(x, idx):
    p = idx.shape[0]
    return pl.pallas_call(
        _gather_kernel,
        grid=(p // GATHER_ROWS,),
        in_specs=[pl.BlockSpec(memory_space=pl.ANY), pl.BlockSpec(memory_space=pl.ANY)],
        out_specs=pl.BlockSpec((GATHER_ROWS, D_MODEL), lambda i: (i, 0)),
        out_shape=jax.ShapeDtypeStruct((p, D_MODEL), x.dtype),
        scratch_shapes=[pltpu.SMEM((GATHER_ROWS,), jnp.int32),
                        pltpu.SemaphoreType.DMA, pltpu.SemaphoreType.DMA],
        compiler_params=_cparams(("arbitrary",)),
        name="moe_gather",
    )(idx, x)


def _expert_kernel(be_ref, bv_ref, x_ref, w1_ref, w3_ref, w2_ref, o_ref, acc_ref):
    del be_ref
    i = pl.program_id(0)
    f = pl.program_id(1)
    valid = bv_ref[i] != 0

    @pl.when(f == 0)
    def _():
        acc_ref[...] = jnp.zeros(acc_ref.shape, F32)

    @pl.when(valid)
    def _():
        xb = x_ref[...].astype(BF16)
        h = jax.nn.silu(_dot(xb, w1_ref[0])) * _dot(xb, w3_ref[0])
        acc_ref[...] += _dot(h.astype(BF16), w2_ref[0])

    @pl.when(f == pl.num_programs(1) - 1)
    def _():
        o_ref[...] = acc_ref[...]


def _expert_ffn(xs, blk_expert, blk_valid, w1, w3, w2, tm, tf):
    p = xs.shape[0]
    d_ff = w1.shape[2]
    nf = d_ff // tf
    fidx = lambda i, f, be, bv: jnp.where(bv[i] != 0, f, nf - 1)
    grid_spec = pltpu.PrefetchScalarGridSpec(
        num_scalar_prefetch=2,
        grid=(p // tm, nf),
        in_specs=[pl.BlockSpec((tm, D_MODEL), lambda i, f, be, bv: (i, 0)),
                  pl.BlockSpec((1, D_MODEL, tf), lambda i, f, be, bv: (be[i], 0, fidx(i, f, be, bv))),
                  pl.BlockSpec((1, D_MODEL, tf), lambda i, f, be, bv: (be[i], 0, fidx(i, f, be, bv))),
                  pl.BlockSpec((1, tf, D_MODEL), lambda i, f, be, bv: (be[i], fidx(i, f, be, bv), 0))],
        out_specs=pl.BlockSpec((tm, D_MODEL), lambda i, f, be, bv: (i, 0)),
        scratch_shapes=[pltpu.VMEM((tm, D_MODEL), F32)],
    )
    return pl.pallas_call(
        _expert_kernel,
        grid_spec=grid_spec,
        out_shape=jax.ShapeDtypeStruct((p, D_MODEL), F32),
        compiler_params=_cparams(("arbitrary", "arbitrary")),
        name="moe_experts",
    )(blk_expert, blk_valid, xs, w1, w3, w2)


def _combine_kernel(pos_hbm, ys_hbm, x_ref, gate_ref, g_ref, b_ref, o_ref,
                    r0_ref, r1_ref, pos_smem, pos_sem, row_sem, *, alpha):
    i = pl.program_id(0)
    tm = x_ref.shape[0]
    n_idx = TOP_K * tm
    cp = pltpu.make_async_copy(pos_hbm.at[pl.ds(pl.multiple_of(i * n_idx, n_idx), n_idx)], pos_smem, pos_sem)
    cp.start()
    cp.wait()

    def row_copies(r):
        return (pltpu.make_async_copy(ys_hbm.at[pl.ds(pos_smem[TOP_K * r], 1)], r0_ref.at[pl.ds(r, 1)], row_sem),
                pltpu.make_async_copy(ys_hbm.at[pl.ds(pos_smem[TOP_K * r + 1], 1)], r1_ref.at[pl.ds(r, 1)], row_sem))

    def issue(r, carry):
        a, b = row_copies(r)
        a.start()
        b.start()
        return carry

    def drain(r, carry):
        a, b = row_copies(r)
        a.wait()
        b.wait()
        return carry

    lax.fori_loop(0, tm, issue, 0, unroll=8)
    lax.fori_loop(0, tm, drain, 0, unroll=8)
    gates = gate_ref[...]
    y = gates[:, 0:1] * r0_ref[...] + gates[:, 1:2] * r1_ref[...]
    o_ref[...] = _ln_rows(alpha * x_ref[...] + y, g_ref[...], b_ref[...])


def _moe_combine(x, ys, pos, gates, g, b, alpha, tm):
    n = x.shape[0]
    return pl.pallas_call(
        functools.partial(_combine_kernel, alpha=alpha),
        grid=(n // tm,),
        in_specs=[pl.BlockSpec(memory_space=pl.ANY), pl.BlockSpec(memory_space=pl.ANY),
                  pl.BlockSpec((tm, D_MODEL), lambda i: (i, 0)),
                  pl.BlockSpec((tm, LANES), lambda i: (i, 0)),
                  pl.BlockSpec((1, D_MODEL), lambda i: (0, 0)),
                  pl.BlockSpec((1, D_MODEL), lambda i: (0, 0))],
        out_specs=pl.BlockSpec((tm, D_MODEL), lambda i: (i, 0)),
        out_shape=jax.ShapeDtypeStruct((n, D_MODEL), F32),
        scratch_shapes=[pltpu.VMEM((tm, D_MODEL), F32), pltpu.VMEM((tm, D_MODEL), F32),
                        pltpu.SMEM((TOP_K * tm,), jnp.int32),
                        pltpu.SemaphoreType.DMA, pltpu.SemaphoreType.DMA],
        compiler_params=_cparams(("arbitrary",)),
        name="moe_combine",
    )(pos, ys, x, gates, g, b)


def _moe_layer(x, w_hi, w_lo, w1, w3, w2, g, b, alpha, tm, tm_e, tf):
    n = x.shape[0]
    a = n * TOP_K
    e_out, gates = _router(x, w_hi, w_lo, tm)
    e_flat = e_out[:, :TOP_K].reshape(a)
    onehot = (e_flat[:, None] == jnp.arange(N_EXPERTS, dtype=jnp.int32)[None, :]).astype(jnp.int32)
    csum = jnp.cumsum(onehot, axis=0)
    counts = csum[-1]
    rank = jnp.sum((csum - 1) * onehot, axis=1)
    padded = (counts + tm_e - 1) // tm_e * tm_e
    pad_end = jnp.cumsum(padded)
    pad_start = pad_end - padded
    dest = (pad_start[e_flat] + rank).astype(jnp.int32)
    p = -(-(a + N_EXPERTS * tm_e) // GATHER_ROWS) * GATHER_ROWS
    p = -(-p // tm_e) * tm_e
    tok_of_slot = jnp.zeros((p,), jnp.int32).at[dest].set(jnp.arange(a, dtype=jnp.int32) // TOP_K)
    blk_start = jnp.arange(p // tm_e, dtype=jnp.int32) * tm_e
    blk_valid = (blk_start < pad_end[-1]).astype(jnp.int32)
    blk_expert = jnp.minimum(jnp.searchsorted(pad_end, jnp.minimum(blk_start, pad_end[-1] - 1), side="right"),
                             N_EXPERTS - 1).astype(jnp.int32)

    xs = _gather_rows(x, tok_of_slot)
    ys = _expert_ffn(xs, blk_expert, blk_valid, w1, w3, w2, tm_e, tf)
    return _moe_combine(x, ys, dest, gates, g, b, alpha, tm)


def _rope_tables(t_max):
    t = jnp.arange(t_max, dtype=jnp.int32)

    def freqs(pos, dim):
        inv = ROPE_THETA ** (-jnp.arange(0, dim, 2, dtype=F32) / dim)
        ang = pos.astype(F32)[:, None] * inv[None, :]
        return jnp.cos(ang), jnp.sin(ang)

    zeros = lambda w: jnp.zeros((t_max, w), F32)
    c, s = freqs(t, MLA_ROPE)
    half = MLA_ROPE // 2
    m_cos = jnp.concatenate([c, c, zeros(LANES - MLA_ROPE)], axis=1)
    m_sa = jnp.concatenate([-s, zeros(LANES - half)], axis=1)
    m_sb = jnp.concatenate([zeros(half), s, zeros(LANES - MLA_ROPE)], axis=1)
    cr, sr = freqs(t // GRID_W, HEAD_DIM // 2)
    cc, sc = freqs(t % GRID_W, HEAD_DIM // 2)
    q = HEAD_DIM // 4
    head_cos = jnp.concatenate([cr, cr, cc, cc], axis=1)
    head_sa = jnp.concatenate([-sr, zeros(q), -sc, zeros(q)], axis=1)
    head_sb = jnp.concatenate([zeros(q), sr, zeros(q), sc], axis=1)
    rep = lambda x: jnp.concatenate([x] * (LANES // HEAD_DIM), axis=1)
    return jnp.concatenate([m_cos, m_sa, m_sb, rep(head_cos), rep(head_sa), rep(head_sb)], axis=1)


def _chunk_heads(w, axis):
    shape = w.shape
    w = w.reshape(shape[:axis] + (HEADS, HEAD_DIM) + shape[axis + 1:])
    w = jnp.take(w, jnp.array(HEAD_PERM), axis=axis)
    return w.reshape(shape)


def _prep_layer(l, w_in, mla_q_norm, mla_kv_norm, w_uq, w_ukv, ax_q_norm, ax_k_norm,
                w_br_mla, w_br_win, w_br_ax, w_out, ln1_g, ln1_b, ln2_g, ln2_b):
    wi = w_in[l]
    sizes = [MLA_Q_RANK, MLA_KV_RANK, MLA_ROPE, QW, KW, KW, QW, KW, KW, N_BRANCH * D_MODEL]
    offs = [0]
    for s in sizes:
        offs.append(offs[-1] + s)
    cols = [wi[:, offs[j]:offs[j + 1]] for j in range(len(sizes))]
    c_q, c_kv, k_r, wq, wk, wv, aq, ak, av, wg = cols
    w_s = jnp.concatenate([c_q, c_kv, k_r, jnp.zeros((D_MODEL, LANES - MLA_ROPE), F32),
                           _chunk_heads(wq, 1), wk, wv, _chunk_heads(aq, 1), ak, av], axis=1).astype(BF16)
    uq = w_uq[l].reshape(MLA_Q_RANK, MLA_HEADS, MLA_QK)
    uq = jnp.pad(uq, ((0, 0), (0, 0), (0, MLA_QK_PAD - MLA_QK))).reshape(MLA_Q_RANK, MLA_HEADS * MLA_QK_PAD)
    ukv = w_ukv[l].reshape(MLA_KV_RANK, MLA_HEADS, MLA_NOPE + MLA_V)
    ukv = jnp.concatenate([ukv[:, :, :MLA_NOPE].reshape(MLA_KV_RANK, -1),
                           ukv[:, :, MLA_NOPE:].reshape(MLA_KV_RANK, -1)], axis=1)
    head_id = jnp.arange(LANES) // HEAD_DIM
    bd = jnp.where(head_id[:, None] == head_id[None, :], 1.0 / HEAD_DIM, 0.0).astype(BF16)
    row = lambda v: v.reshape(1, -1).astype(F32)
    return dict(
        w_s=w_s, w_g=wg.astype(BF16), w_uq=uq.astype(BF16), w_ukv=ukv.astype(BF16),
        g_q=row(mla_q_norm[l]), g_kv=row(mla_kv_norm[l]),
        g_aq=row(jnp.tile(ax_q_norm[l], LANES // HEAD_DIM)), g_ak=row(jnp.tile(ax_k_norm[l], LANES // HEAD_DIM)),
        bd=bd,
        w_br_mla=w_br_mla[l].astype(BF16),
        w_br_win=_chunk_heads(w_br_win[l], 0).astype(BF16),
        w_br_ax=_chunk_heads(w_br_ax[l], 0).astype(BF16),
        w_out=w_out[l].astype(BF16),
        ln1_g=row(ln1_g[l]), ln1_b=row(ln1_b[l]), ln2_g=row(ln2_g[l]), ln2_b=row(ln2_b[l]),
    )


def kernel(x_prompt, x_sample, emb_ln_g, emb_ln_b, w_in, mla_q_norm, mla_kv_norm, w_uq, w_ukv, win_sink,
           ax_q_norm, ax_k_norm, w_br_mla, w_br_win, w_br_ax, w_out, ln1_g, ln1_b, ln2_g, ln2_b,
           ffn_w1, ffn_w3, ffn_w2, moe_router, moe_w1, moe_w3, moe_w2):
    depth = w_in.shape[0]
    alpha = (2 * depth) ** 0.25
    groups = []
    off = 0
    for xg in (x_prompt, x_sample):
        bsz, seq, _ = xg.shape
        assert off % seq == 0 and seq % GRID_W == 0
        groups.append((off, bsz, seq))
        off += bsz * seq
    n = off
    seq_min = min(g[2] for g in groups)
    seq_max = max(g[2] for g in groups)
    tm = _pick_tile(seq_min, 512)
    tm_e = 512
    x = jnp.concatenate([x_prompt.reshape(-1, D_MODEL), x_sample.reshape(-1, D_MODEL)], axis=0)

    tab = _rope_tables(seq_max)
    pos_blk = jnp.concatenate([jnp.tile(jnp.arange(seq // tm, dtype=jnp.int32), bsz)
                               for (_, bsz, seq) in groups])

    x = _embed_ln(x, emb_ln_g, emb_ln_b, tm)
    for l in range(depth):
        lw = _prep_layer(l, w_in, mla_q_norm, mla_kv_norm, w_uq, w_ukv, ax_q_norm, ax_k_norm,
                         w_br_mla, w_br_win, w_br_ax, w_out, ln1_g, ln1_b, ln2_g, ln2_b)
        qm, km, vm, wq, wk, wv, aq, ak, av = _in_proj(x, pos_blk, lw, tab, tm)
        sink = win_sink[l].astype(F32)
        o_a = jnp.concatenate([_mla_attention(qm, km, vm, *g) for g in groups], axis=0)
        o_b = jnp.concatenate([_window_attention(wq, wk, wv, sink, *g) for g in groups], axis=0)
        o_c = jnp.concatenate([_axial_attention(aq, ak, av, *g) for g in groups], axis=0)
        x = _merge(x, o_a, o_b, o_c, lw, alpha, tm)
        i = l // 2
        if l % 2 == 0:
            d_ff = ffn_w1.shape[2]
            x = _dense_ffn(x, ffn_w1[i].astype(BF16), ffn_w3[i].astype(BF16), ffn_w2[i].astype(BF16),
                           lw["ln2_g"], lw["ln2_b"], alpha, tm, _pick_tile(d_ff, d_ff // 2))
        else:
            router = jnp.pad(moe_router[i], ((0, 0), (0, LANES - N_EXPERTS)))
            r_hi = router.astype(BF16)
            r_lo = (router - r_hi.astype(F32)).astype(BF16)
            d_ffe = moe_w1.shape[3]
            x = _moe_layer(x, r_hi, r_lo, moe_w1[i].astype(BF16), moe_w3[i].astype(BF16),
                           moe_w2[i].astype(BF16), lw["ln2_g"], lw["ln2_b"], alpha, tm, tm_e,
                           _pick_tile(d_ffe, d_ffe // 4))
    n_p = x_prompt.shape[0] * x_prompt.shape[1]
    return (x[:n_p].reshape(x_prompt.shape), x[n_p:].reshape(x_sample.shape))
```

```python
import functools
import math

import jax
import jax.numpy as jnp
from jax import lax
from jax.experimental import pallas as pl
from jax.experimental.pallas import tpu as pltpu

F32 = jnp.float32
BF16 = jnp.bfloat16

D_MODEL = 1024
GRID_W = 64
ROPE_THETA = 10000.0
RMS_EPS = 1e-6
LN_EPS = 1e-5
NEG_INF = -1e30

MLA_HEADS = 4
MLA_Q_RANK = 384
MLA_KV_RANK = 256
MLA_NOPE = 128
MLA_ROPE = 64
MLA_V = 128
MLA_QK = MLA_NOPE + MLA_ROPE

HEADS = 8
KV_HEADS = 2
HEAD_DIM = 64
GROUP = HEADS // KV_HEADS
WINDOW = 128
Q_BLOCK = 128
SPAN = Q_BLOCK + 2 * WINDOW

N_BRANCH = 3
N_EXPERTS = 8
TOP_K = 2

LANES = 128
BF16_ROWS = 16
MXU_DIM = 256
VMEM_LIMIT = 56 * 1024 * 1024

MLA_QK_PAD = MXU_DIM
ATT_V_PAD = MXU_DIM
QW = HEADS * HEAD_DIM
KW = KV_HEADS * HEAD_DIM
C_CQ = 0
C_CKV = C_CQ + MLA_Q_RANK
C_KR = C_CKV + MLA_KV_RANK
C_WQ = C_KR + LANES
C_WK = C_WQ + QW
C_WV = C_WK + KW
C_AQ = C_WV + KW
C_AK = C_AQ + QW
C_AV = C_AK + KW
C_END = C_AV + KW
TAB_W = 6 * LANES
LOG2E = math.log2(math.e)
ATT_TK = 256

HEAD_PERM = [h for c in range(GROUP) for h in (c, GROUP + c)]


def _cparams(sem, vmem=VMEM_LIMIT, flags=None):
    return pltpu.CompilerParams(dimension_semantics=sem, vmem_limit_bytes=vmem, flags=flags)


ATT_FLAGS = None


def _pick_tile(n, pref):
    t = min(pref, n)
    while n % t:
        t //= 2
    return t


def _ln_rows(y, g, b):
    mu = jnp.mean(y, axis=-1, keepdims=True)
    d = y - mu
    var = jnp.mean(d * d, axis=-1, keepdims=True)
    return d * lax.rsqrt(var + LN_EPS) * g + b


def _rms_rows(x, g):
    ms = jnp.mean(x * x, axis=-1, keepdims=True)
    return x * lax.rsqrt(ms + RMS_EPS) * g


def _rope_chunk(x, cos, sin_a, sin_b, half):
    return (x * cos + pltpu.roll(x, LANES - half, 1) * sin_a + pltpu.roll(x, half, 1) * sin_b)


def _dot(a, b):
    return jnp.dot(a, b, preferred_element_type=F32)


def _dot_nt(a, b):
    return lax.dot_general(a, b, (((1,), (1,)), ((), ())), preferred_element_type=F32)


def _split_dot(x, w_bf16):
    hi = x.astype(BF16)
    lo = (x - hi.astype(F32)).astype(BF16)
    return _dot(hi, w_bf16) + _dot(lo, w_bf16)


def _ln_kernel(x_ref, g_ref, b_ref, o_ref):
    o_ref[...] = _ln_rows(x_ref[...], g_ref[...], b_ref[...])


def _embed_ln(x, g, b, tm):
    n = x.shape[0]
    return pl.pallas_call(
        _ln_kernel,
        grid=(n // tm,),
        in_specs=[pl.BlockSpec((tm, D_MODEL), lambda i: (i, 0)),
                  pl.BlockSpec((1, D_MODEL), lambda i: (0, 0)),
                  pl.BlockSpec((1, D_MODEL), lambda i: (0, 0))],
        out_specs=pl.BlockSpec((tm, D_MODEL), lambda i: (i, 0)),
        out_shape=jax.ShapeDtypeStruct((n, D_MODEL), F32),
        compiler_params=_cparams(("parallel",)),
        name="embed_ln",
    )(x, g.reshape(1, -1), b.reshape(1, -1))


def _proj_kernel(pos_ref, x_ref, ws_ref, wuq_ref, wukv_ref, gq_ref, gkv_ref, gaq_ref, gak_ref,
                 bd_ref, tab_ref,
                 qm_ref, km_ref, vm_ref, wq_ref, wk_ref, wv_ref, aq_ref, ak_ref, av_ref):
    del pos_ref
    tm = x_ref.shape[0]
    x = x_ref[...].astype(BF16)
    res = _dot(x, ws_ref[...])

    m_cos = tab_ref[:, 0 * LANES:1 * LANES]
    m_sa = tab_ref[:, 1 * LANES:2 * LANES]
    m_sb = tab_ref[:, 2 * LANES:3 * LANES]
    a_cos = tab_ref[:, 3 * LANES:4 * LANES]
    a_sa = tab_ref[:, 4 * LANES:5 * LANES]
    a_sb = tab_ref[:, 5 * LANES:6 * LANES]
    ones_col = jnp.where(lax.broadcasted_iota(jnp.int32, (tm, LANES), 1) == 0, 1.0, 0.0).astype(BF16)

    c_q = _rms_rows(res[:, C_CQ:C_CKV], gq_ref[...]).astype(BF16)
    q = _dot(c_q, wuq_ref[...])
    q_scale = MLA_QK ** -0.5 * LOG2E
    for h in range(MLA_HEADS):
        base = h * MLA_QK_PAD
        qm_ref[:, base:base + LANES] = (q[:, base:base + LANES] * q_scale).astype(BF16)
        roped = _rope_chunk(q[:, base + LANES:base + 2 * LANES], m_cos, m_sa, m_sb, MLA_ROPE // 2)
        qm_ref[:, base + LANES:base + 2 * LANES] = (roped * q_scale).astype(BF16)

    c_kv = _rms_rows(res[:, C_CKV:C_KR], gkv_ref[...]).astype(BF16)
    kv = _dot(c_kv, wukv_ref[...])
    k_rope = _rope_chunk(res[:, C_KR:C_WQ], m_cos, m_sa, m_sb, MLA_ROPE // 2).astype(BF16)
    for h in range(MLA_HEADS):
        base = h * MLA_QK_PAD
        km_ref[:, base:base + LANES] = kv[:, h * LANES:(h + 1) * LANES].astype(BF16)
        km_ref[:, base + LANES:base + 2 * LANES] = k_rope
    v_off = MLA_HEADS * MLA_NOPE
    for r in range(tm // ATT_TK):
        rows = slice(r * ATT_TK, (r + 1) * ATT_TK)
        for h in range(MLA_HEADS):
            vm_ref[r, h * MLA_V:(h + 1) * MLA_V, :] = (
                kv[rows, v_off + h * MLA_V:v_off + (h + 1) * MLA_V].T.astype(BF16))
        av_ref[r] = res[rows, C_AV:C_END].T.astype(BF16)

    wq_ref[...] = (res[:, C_WQ:C_WK] * (HEAD_DIM ** -0.5)).astype(BF16)
    wk_ref[...] = res[:, C_WK:C_WV].astype(BF16)
    wv_ref[:, 0:LANES] = res[:, C_WV:C_AQ].astype(BF16)
    wv_ref[:, LANES:2 * LANES] = ones_col

    bd = bd_ref[...]

    def norm_rope(xc, g):
        ms = _split_dot(xc * xc, bd)
        xn = xc * lax.rsqrt(ms + RMS_EPS) * g
        return _rope_chunk(xn, a_cos, a_sa, a_sb, HEAD_DIM // 4)

    for c in range(QW // LANES):
        xc = res[:, C_AQ + c * LANES:C_AQ + (c + 1) * LANES]
        aq_ref[:, c * LANES:(c + 1) * LANES] = (
            norm_rope(xc, gaq_ref[...]) * (HEAD_DIM ** -0.5 * LOG2E)).astype(BF16)
    ak_ref[...] = norm_rope(res[:, C_AK:C_AV], gak_ref[...]).astype(BF16)


def _in_proj(x, pos_blk, lw, tab, tm):
    n = x.shape[0]
    const = lambda shape: pl.BlockSpec(shape, lambda i, p: (0, 0))
    rows = lambda w: pl.BlockSpec((tm, w), lambda i, p: (i, 0))
    out_widths = [MLA_HEADS * MLA_QK_PAD, MLA_HEADS * MLA_QK_PAD, (MLA_HEADS * MLA_V,),
                  QW, KW, ATT_V_PAD, QW, KW, (KW,)]

    def out_spec(w):
        if isinstance(w, tuple):
            return pl.BlockSpec((tm // ATT_TK, w[0], ATT_TK), lambda i, p: (i, 0, 0))
        return rows(w)

    def out_shape(w):
        if isinstance(w, tuple):
            return jax.ShapeDtypeStruct((n // ATT_TK, w[0], ATT_TK), BF16)
        return jax.ShapeDtypeStruct((n, w), BF16)

    grid_spec = pltpu.PrefetchScalarGridSpec(
        num_scalar_prefetch=1,
        grid=(n // tm,),
        in_specs=[rows(D_MODEL),
                  const((D_MODEL, C_END)),
                  const((MLA_Q_RANK, MLA_HEADS * MLA_QK_PAD)),
                  const((MLA_KV_RANK, 2 * MLA_HEADS * LANES)),
                  const((1, MLA_Q_RANK)), const((1, MLA_KV_RANK)),
                  const((1, LANES)), const((1, LANES)),
                  const((LANES, LANES)),
                  pl.BlockSpec((tm, TAB_W), lambda i, p: (p[i], 0))],
        out_specs=[out_spec(w) for w in out_widths],
    )
    return pl.pallas_call(
        _proj_kernel,
        grid_spec=grid_spec,
        out_shape=[out_shape(w) for w in out_widths],
        compiler_params=_cparams(("parallel",)),
        name="in_proj",
    )(pos_blk, x, lw["w_s"], lw["w_uq"], lw["w_ukv"], lw["g_q"], lw["g_kv"], lw["g_aq"], lw["g_ak"],
      lw["bd"], tab)


def _flash_t(qt_sc, k_ref, vt_ref, s0_sc, s1_sc, acc_sc, seq, v_rows):
    n_blk = acc_sc.shape[0]
    nk = seq // ATT_TK
    assert nk >= 2 and nk % 2 == 0
    acc_sc[...] = jnp.zeros(acc_sc.shape, F32)
    stat = lambda v: tuple(jnp.full((1, MXU_DIM), v, F32) for _ in range(n_blk))

    def scores(j, s_ref):
        k = k_ref[pl.ds(pl.multiple_of(j * ATT_TK, ATT_TK), ATT_TK), :]
        for b in range(n_blk):
            s_ref[b] = _dot(k, qt_sc[:, b * MXU_DIM:(b + 1) * MXU_DIM])

    ones_rows = jnp.where(lax.broadcasted_iota(jnp.int32, (BF16_ROWS, ATT_TK), 0) == 0, 1.0, 0.0).astype(BF16)

    def consume(j, s_ref, ms):
        ms = list(ms)
        for b in range(n_blk):
            s = s_ref[b]
            m_new = jnp.maximum(ms[b], jnp.max(s, axis=0, keepdims=True))
            alpha = jnp.exp2(ms[b] - m_new)
            p = jnp.exp2((s - m_new).astype(BF16))
            ms[b] = m_new
            r0, r1 = v_rows[b]
            vt = jnp.concatenate([vt_ref[j, r0:r1, :], ones_rows], axis=0)
            acc_sc[b] = acc_sc[b] * alpha + _dot(vt, p)
        return tuple(ms)

    scores(0, s0_sc)

    def pair(i, carry):
        scores(2 * i + 1, s1_sc)
        carry = consume(2 * i, s0_sc, carry)
        scores(2 * i + 2, s0_sc)
        return consume(2 * i + 1, s1_sc, carry)

    carry = lax.fori_loop(0, nk // 2 - 1, pair, stat(NEG_INF), unroll=(nk <= 8))
    scores(nk - 1, s1_sc)
    carry = consume(nk - 2, s0_sc, carry)
    consume(nk - 1, s1_sc, carry)


def _stack_heads(q):
    rows = q.shape[0]
    lo = lax.broadcasted_iota(jnp.int32, (rows, LANES), 1) < HEAD_DIM
    zero = jnp.zeros((rows, LANES), q.dtype)
    chunks = [q[:, c * LANES:(c + 1) * LANES] for c in range(GROUP)]
    parts = [jnp.where(lo, ch, zero) for ch in chunks] + [jnp.where(lo, zero, ch) for ch in chunks]
    return jnp.concatenate(parts, axis=0)


def _unstack_heads(o, rows, o_ref):
    lo = lax.broadcasted_iota(jnp.int32, (rows, LANES), 1) < HEAD_DIM
    for c in range(GROUP):
        top = o[c * rows:(c + 1) * rows]
        bot = o[(GROUP + c) * rows:(GROUP + c + 1) * rows]
        o_ref[:, c * LANES:(c + 1) * LANES] = jnp.where(lo, top, bot).astype(o_ref.dtype)


def _gqa_kernel(q_ref, k_ref, vt_ref, o_ref, qt_sc, s0_sc, s1_sc, acc_sc, *, seq):
    tq = q_ref.shape[0]
    lo = lax.broadcasted_iota(jnp.int32, (LANES, tq), 0) < HEAD_DIM
    for c in range(GROUP):
        qt = q_ref[:, c * LANES:(c + 1) * LANES].astype(F32).T
        qt_sc[:, c * tq:(c + 1) * tq] = jnp.where(lo, qt, 0.0).astype(BF16)
        qt_sc[:, (GROUP + c) * tq:(GROUP + c + 1) * tq] = jnp.where(lo, 0.0, qt).astype(BF16)
    heads_per_blk = MXU_DIM // tq
    v_rows = [((b * heads_per_blk) // GROUP * HEAD_DIM, ((b * heads_per_blk) // GROUP + 1) * HEAD_DIM)
              for b in range(acc_sc.shape[0])]
    _flash_t(qt_sc, k_ref, vt_ref, s0_sc, s1_sc, acc_sc, seq, v_rows)

    def head_out(h):
        b, off = divmod(h * tq, MXU_DIM)
        return acc_sc[b, 0:HEAD_DIM, off:off + tq] / acc_sc[b, HEAD_DIM:HEAD_DIM + 1, off:off + tq]

    for c in range(GROUP):
        ot = jnp.concatenate([head_out(c), head_out(GROUP + c)], axis=0)
        o_ref[:, c * LANES:(c + 1) * LANES] = ot.T.astype(o_ref.dtype)


def _axial_attention(q, k, vt, tok_off, batch, seq):
    tq = LANES
    nq = seq // tq
    nkb = seq // ATT_TK
    qb, sb = tok_off // tq, tok_off // seq
    n_blk = HEADS * tq // MXU_DIM
    return pl.pallas_call(
        functools.partial(_gqa_kernel, seq=seq),
        grid=(batch, nq),
        in_specs=[pl.BlockSpec((tq, QW), lambda b, i: (qb + b * nq + i, 0)),
                  pl.BlockSpec((seq, KW), lambda b, i: (sb + b, 0)),
                  pl.BlockSpec((nkb, KW, ATT_TK), lambda b, i: (sb + b, 0, 0))],
        out_specs=pl.BlockSpec((tq, QW), lambda b, i: (b * nq + i, 0)),
        out_shape=jax.ShapeDtypeStruct((batch * seq, QW), BF16),
        scratch_shapes=[pltpu.VMEM((KW, HEADS * tq), BF16),
                        pltpu.VMEM((n_blk, ATT_TK, MXU_DIM), F32),
                        pltpu.VMEM((n_blk, ATT_TK, MXU_DIM), F32),
                        pltpu.VMEM((n_blk, HEAD_DIM + BF16_ROWS, MXU_DIM), F32)],
        compiler_params=_cparams(("parallel", "arbitrary"), flags=ATT_FLAGS),
        name="axial_attention",
    )(q, k, vt)


def _mla_kernel(q_ref, k_ref, vt_ref, o_ref, qt_sc, s0_sc, s1_sc, acc_sc, *, seq):
    n_blk = acc_sc.shape[0]
    for b in range(n_blk):
        qt_sc[:, b * MXU_DIM:(b + 1) * MXU_DIM] = (
            q_ref[b * MXU_DIM:(b + 1) * MXU_DIM, :].astype(F32).T.astype(BF16))
    _flash_t(qt_sc, k_ref, vt_ref, s0_sc, s1_sc, acc_sc, seq, [(0, MLA_V)] * n_blk)
    for b in range(n_blk):
        o = acc_sc[b, 0:MLA_V, :] / acc_sc[b, MLA_V:MLA_V + 1, :]
        o_ref[b * MXU_DIM:(b + 1) * MXU_DIM, :] = o.T.astype(o_ref.dtype)


def _mla_attention(q, k, vt, tok_off, batch, seq):
    tq = _pick_tile(seq, 4 * MXU_DIM)
    assert tq % MXU_DIM == 0
    nq = seq // tq
    nkb = seq // ATT_TK
    qb, sb = tok_off // tq, tok_off // seq
    return pl.pallas_call(
        functools.partial(_mla_kernel, seq=seq),
        grid=(batch, MLA_HEADS, nq),
        in_specs=[pl.BlockSpec((tq, MLA_QK_PAD), lambda b, h, i: (qb + b * nq + i, h)),
                  pl.BlockSpec((seq, MLA_QK_PAD), lambda b, h, i: (sb + b, h)),
                  pl.BlockSpec((nkb, MLA_V, ATT_TK), lambda b, h, i: (sb + b, h, 0))],
        out_specs=pl.BlockSpec((tq, MLA_V), lambda b, h, i: (b * nq + i, h)),
        out_shape=jax.ShapeDtypeStruct((batch * seq, MLA_HEADS * MLA_V), BF16),
        scratch_shapes=[pltpu.VMEM((MLA_QK_PAD, tq), BF16),
                        pltpu.VMEM((tq // MXU_DIM, ATT_TK, MXU_DIM), F32),
                        pltpu.VMEM((tq // MXU_DIM, ATT_TK, MXU_DIM), F32),
                        pltpu.VMEM((tq // MXU_DIM, MLA_V + BF16_ROWS, MXU_DIM), F32)],
        compiler_params=_cparams(("parallel", "parallel", "arbitrary"), flags=ATT_FLAGS),
        name="mla_attention",
    )(q, k, vt)


def _window_kernel(sink_ref, q_ref, k_ref, v_ref, o_ref, *, seq):
    i = pl.program_id(1)
    start = i * Q_BLOCK
    kstart = pl.multiple_of(jnp.clip(start - WINDOW, 0, seq - SPAN), Q_BLOCK)
    k = k_ref[pl.ds(kstart, SPAN), :]
    v = v_ref[pl.ds(kstart, SPAN), :]
    s_all = _dot_nt(_stack_heads(q_ref[...]), k)
    rel = ((kstart + lax.broadcasted_iota(jnp.int32, (Q_BLOCK, SPAN), 1))
           - (start + lax.broadcasted_iota(jnp.int32, (Q_BLOCK, SPAN), 0)))
    dist_i = jnp.abs(rel)
    dist = dist_i.astype(F32)
    valid = dist_i <= WINDOW
    outs = []
    for h in range(HEADS):
        slope = 2.0 ** (-8.0 * (h + 1) / HEADS)
        sink = sink_ref[h]
        logits = jnp.where(valid, s_all[h * Q_BLOCK:(h + 1) * Q_BLOCK] - slope * dist, NEG_INF)
        m = jnp.maximum(jnp.max(logits, axis=1, keepdims=True), sink)
        p = jnp.exp(logits - m)
        pv = _dot(p.astype(BF16), v)
        denom = pv[:, LANES:LANES + 1] + jnp.exp(sink - m)
        outs.append(pv[:, 0:LANES] / denom)
    _unstack_heads(jnp.concatenate(outs, axis=0), Q_BLOCK, o_ref)


def _window_attention(q, k, v, sink, tok_off, batch, seq):
    assert seq >= SPAN and seq % Q_BLOCK == 0
    nq = seq // Q_BLOCK
    qb, sb = tok_off // Q_BLOCK, tok_off // seq
    return pl.pallas_call(
        functools.partial(_window_kernel, seq=seq),
        grid=(batch, nq),
        in_specs=[pl.BlockSpec(memory_space=pltpu.SMEM),
                  pl.BlockSpec((Q_BLOCK, QW), lambda b, i: (qb + b * nq + i, 0)),
                  pl.BlockSpec((seq, KW), lambda b, i: (sb + b, 0)),
                  pl.BlockSpec((seq, ATT_V_PAD), lambda b, i: (sb + b, 0))],
        out_specs=pl.BlockSpec((Q_BLOCK, QW), lambda b, i: (b * nq + i, 0)),
        out_shape=jax.ShapeDtypeStruct((batch * seq, QW), BF16),
        compiler_params=_cparams(("parallel", "arbitrary")),
        name="window_attention",
    )(sink, q, k, v)


def _merge_kernel(x_ref, oa_ref, ob_ref, oc_ref, wg_ref, wa_ref, wb_ref, wc_ref, wo_ref, g_ref, b_ref,
                  o_ref, *, alpha):
    x = x_ref[...]
    xb = x.astype(BF16)
    merged = None
    for idx, (o_br, w_br) in enumerate(((oa_ref, wa_ref), (ob_ref, wb_ref), (oc_ref, wc_ref))):
        gate = jax.nn.sigmoid(_dot(xb, wg_ref[:, idx * D_MODEL:(idx + 1) * D_MODEL]))
        term = gate * _dot(o_br[...], w_br[...])
        merged = term if merged is None else merged + term
    m = _dot(merged.astype(BF16), wo_ref[...])
    o_ref[...] = _ln_rows(alpha * x + m, g_ref[...], b_ref[...])


def _merge(x, o_a, o_b, o_c, lw, alpha, tm):
    n = x.shape[0]
    const = lambda shape: pl.BlockSpec(shape, lambda i: (0, 0))
    rows = lambda w: pl.BlockSpec((tm, w), lambda i: (i, 0))
    return pl.pallas_call(
        functools.partial(_merge_kernel, alpha=alpha),
        grid=(n // tm,),
        in_specs=[rows(D_MODEL), rows(MLA_HEADS * MLA_V), rows(QW), rows(QW),
                  const((D_MODEL, N_BRANCH * D_MODEL)),
                  const((MLA_HEADS * MLA_V, D_MODEL)), const((QW, D_MODEL)), const((QW, D_MODEL)),
                  const((D_MODEL, D_MODEL)), const((1, D_MODEL)), const((1, D_MODEL))],
        out_specs=rows(D_MODEL),
        out_shape=jax.ShapeDtypeStruct((n, D_MODEL), F32),
        compiler_params=_cparams(("parallel",)),
        name="gated_merge",
    )(x, o_a, o_b, o_c, lw["w_g"], lw["w_br_mla"], lw["w_br_win"], lw["w_br_ax"], lw["w_out"],
      lw["ln1_g"], lw["ln1_b"])


def _ffn_kernel(x_ref, w1_ref, w3_ref, w2_ref, g_ref, b_ref, o_ref, acc_ref, *, alpha):
    f = pl.program_id(1)

    @pl.when(f == 0)
    def _():
        acc_ref[...] = jnp.zeros(acc_ref.shape, F32)

    xb = x_ref[...].astype(BF16)
    h = jax.nn.silu(_dot(xb, w1_ref[...])) * _dot(xb, w3_ref[...])
    acc_ref[...] += _dot(h.astype(BF16), w2_ref[...])

    @pl.when(f == pl.num_programs(1) - 1)
    def _():
        o_ref[...] = _ln_rows(alpha * x_ref[...] + acc_ref[...], g_ref[...], b_ref[...])


def _dense_ffn(x, w1, w3, w2, g, b, alpha, tm, tf):
    n = x.shape[0]
    d_ff = w1.shape[1]
    return pl.pallas_call(
        functools.partial(_ffn_kernel, alpha=alpha),
        grid=(n // tm, d_ff // tf),
        in_specs=[pl.BlockSpec((tm, D_MODEL), lambda i, f: (i, 0)),
                  pl.BlockSpec((D_MODEL, tf), lambda i, f: (0, f)),
                  pl.BlockSpec((D_MODEL, tf), lambda i, f: (0, f)),
                  pl.BlockSpec((tf, D_MODEL), lambda i, f: (f, 0)),
                  pl.BlockSpec((1, D_MODEL), lambda i, f: (0, 0)),
                  pl.BlockSpec((1, D_MODEL), lambda i, f: (0, 0))],
        out_specs=pl.BlockSpec((tm, D_MODEL), lambda i, f: (i, 0)),
        out_shape=jax.ShapeDtypeStruct((n, D_MODEL), F32),
        scratch_shapes=[pltpu.VMEM((tm, D_MODEL), F32)],
        compiler_params=_cparams(("parallel", "arbitrary")),
        name="dense_ffn",
    )(x, w1, w3, w2, g, b)


def _router_kernel(x_ref, whi_ref, wlo_ref, e_ref, g_ref):
    x = x_ref[...]
    hi = x.astype(BF16)
    lo = (x - hi.astype(F32)).astype(BF16)
    logits = _dot(hi, whi_ref[...]) + _dot(lo, whi_ref[...]) + _dot(hi, wlo_ref[...])
    lane = lax.broadcasted_iota(jnp.int32, logits.shape, 1)
    logits = jnp.where(lane < N_EXPERTS, logits, NEG_INF)
    t1 = jnp.max(logits, axis=1, keepdims=True)
    e1 = jnp.min(jnp.where(logits == t1, lane, LANES), axis=1, keepdims=True)
    rest = jnp.where(lane == e1, NEG_INF, logits)
    t2 = jnp.max(rest, axis=1, keepdims=True)
    e2 = jnp.min(jnp.where(rest == t2, lane, LANES), axis=1, keepdims=True)
    w = jnp.exp(t2 - t1)
    g1 = 1.0 / (1.0 + w)
    g2 = w / (1.0 + w)
    e_ref[...] = jnp.where(lane == 0, e1, jnp.where(lane == 1, e2, 0))
    g_ref[...] = jnp.where(lane == 0, g1, jnp.where(lane == 1, g2, 0.0))


def _router(x, w_hi, w_lo, tm):
    n = x.shape[0]
    return pl.pallas_call(
        _router_kernel,
        grid=(n // tm,),
        in_specs=[pl.BlockSpec((tm, D_MODEL), lambda i: (i, 0)),
                  pl.BlockSpec((D_MODEL, LANES), lambda i: (0, 0)),
                  pl.BlockSpec((D_MODEL, LANES), lambda i: (0, 0))],
        out_specs=[pl.BlockSpec((tm, LANES), lambda i: (i, 0)),
                   pl.BlockSpec((tm, LANES), lambda i: (i, 0))],
        out_shape=[jax.ShapeDtypeStruct((n, LANES), jnp.int32),
                   jax.ShapeDtypeStruct((n, LANES), F32)],
        compiler_params=_cparams(("parallel",)),
        name="moe_router",
    )(x, w_hi, w_lo)


GATHER_ROWS = 1024


def _gather_kernel(idx_hbm, x_hbm, o_ref, idx_smem, idx_sem, row_sem):
    i = pl.program_id(0)
    rows = o_ref.shape[0]
    cp = pltpu.make_async_copy(idx_hbm.at[pl.ds(pl.multiple_of(i * rows, rows), rows)], idx_smem, idx_sem)
    cp.start()
    cp.wait()

    def row_copy(r):
        return pltpu.make_async_copy(x_hbm.at[pl.ds(idx_smem[r], 1)], o_ref.at[pl.ds(r, 1)], row_sem)

    def issue(r, carry):
        row_copy(r).start()
        return carry

    def drain(r, carry):
        row_copy(r).wait()
        return carry

    lax.fori_loop(0, rows, issue, 0, unroll=8)
    lax.fori_loop(0, rows, drain, 0, unroll=8)


def _gather_rows(x, idx):
    p = idx.shape[0]
    return pl.pallas_call(
        _gather_kernel,
        grid=(p // GATHER_ROWS,),
        in_specs=[pl.BlockSpec(memory_space=pl.ANY), pl.BlockSpec(memory_space=pl.ANY)],
        out_specs=pl.BlockSpec((GATHER_ROWS, D_MODEL), lambda i: (i, 0)),
        out_shape=jax.ShapeDtypeStruct((p, D_MODEL), x.dtype),
        scratch_shapes=[pltpu.SMEM((GATHER_ROWS,), jnp.int32),
                        pltpu.SemaphoreType.DMA, pltpu.SemaphoreType.DMA],
        compiler_params=_cparams(("arbitrary",)),
        name="moe_gather",
    )(idx, x)


def _expert_kernel(be_ref, bv_ref, x_ref, w1_ref, w3_ref, w2_ref, o_ref, acc_ref):
    del be_ref
    i = pl.program_id(0)
    f = pl.program_id(1)
    valid = bv_ref[i] != 0

    @pl.when(f == 0)
    def _():
        acc_ref[...] = jnp.zeros(acc_ref.shape, F32)

    @pl.when(valid)
    def _():
        xb = x_ref[...].astype(BF16)
        h = jax.nn.silu(_dot(xb, w1_ref[0])) * _dot(xb, w3_ref[0])
        acc_ref[...] += _dot(h.astype(BF16), w2_ref[0])

    @pl.when(f == pl.num_programs(1) - 1)
    def _():
        o_ref[...] = acc_ref[...]


def _expert_ffn(xs, blk_expert, blk_valid, w1, w3, w2, tm, tf):
    p = xs.shape[0]
    d_ff = w1.shape[2]
    nf = d_ff // tf
    fidx = lambda i, f, be, bv: jnp.where(bv[i] != 0, f, nf - 1)
    grid_spec = pltpu.PrefetchScalarGridSpec(
        num_scalar_prefetch=2,
        grid=(p // tm, nf),
        in_specs=[pl.BlockSpec((tm, D_MODEL), lambda i, f, be, bv: (i, 0)),
                  pl.BlockSpec((1, D_MODEL, tf), lambda i, f, be, bv: (be[i], 0, fidx(i, f, be, bv))),
                  pl.BlockSpec((1, D_MODEL, tf), lambda i, f, be, bv: (be[i], 0, fidx(i, f, be, bv))),
                  pl.BlockSpec((1, tf, D_MODEL), lambda i, f, be, bv: (be[i], fidx(i, f, be, bv), 0))],
        out_specs=pl.BlockSpec((tm, D_MODEL), lambda i, f, be, bv: (i, 0)),
        scratch_shapes=[pltpu.VMEM((tm, D_MODEL), F32)],
    )
    return pl.pallas_call(
        _expert_kernel,
        grid_spec=grid_spec,
        out_shape=jax.ShapeDtypeStruct((p, D_MODEL), F32),
        compiler_params=_cparams(("arbitrary", "arbitrary")),
        name="moe_experts",
    )(blk_expert, blk_valid, xs, w1, w3, w2)


def _combine_kernel(pos_hbm, ys_hbm, x_ref, gate_ref, g_ref, b_ref, o_ref,
                    r0_ref, r1_ref, pos_smem, pos_sem, row_sem, *, alpha):
    i = pl.program_id(0)
    tm = x_ref.shape[0]
    n_idx = TOP_K * tm
    cp = pltpu.make_async_copy(pos_hbm.at[pl.ds(pl.multiple_of(i * n_idx, n_idx), n_idx)], pos_smem, pos_sem)
    cp.start()
    cp.wait()

    def row_copies(r):
        return (pltpu.make_async_copy(ys_hbm.at[pl.ds(pos_smem[TOP_K * r], 1)], r0_ref.at[pl.ds(r, 1)], row_sem),
                pltpu.make_async_copy(ys_hbm.at[pl.ds(pos_smem[TOP_K * r + 1], 1)], r1_ref.at[pl.ds(r, 1)], row_sem))

    def issue(r, carry):
        a, b = row_copies(r)
        a.start()
        b.start()
        return carry

    def drain(r, carry):
        a, b = row_copies(r)
        a.wait()
        b.wait()
        return carry

    lax.fori_loop(0, tm, issue, 0, unroll=8)
    lax.fori_loop(0, tm, drain, 0, unroll=8)
    gates = gate_ref[...]
    y = gates[:, 0:1] * r0_ref[...] + gates[:, 1:2] * r1_ref[...]
    o_ref[...] = _ln_rows(alpha * x_ref[...] + y, g_ref[...], b_ref[...])


def _moe_combine(x, ys, pos, gates, g, b, alpha, tm):
    n = x.shape[0]
    return pl.pallas_call(
        functools.partial(_combine_kernel, alpha=alpha),
        grid=(n // tm,),
        in_specs=[pl.BlockSpec(memory_space=pl.ANY), pl.BlockSpec(memory_space=pl.ANY),
                  pl.BlockSpec((tm, D_MODEL), lambda i: (i, 0)),
                  pl.BlockSpec((tm, LANES), lambda i: (i, 0)),
                  pl.BlockSpec((1, D_MODEL), lambda i: (0, 0)),
                  pl.BlockSpec((1, D_MODEL), lambda i: (0, 0))],
        out_specs=pl.BlockSpec((tm, D_MODEL), lambda i: (i, 0)),
        out_shape=jax.ShapeDtypeStruct((n, D_MODEL), F32),
        scratch_shapes=[pltpu.VMEM((tm, D_MODEL), F32), pltpu.VMEM((tm, D_MODEL), F32),
                        pltpu.SMEM((TOP_K * tm,), jnp.int32),
                        pltpu.SemaphoreType.DMA, pltpu.SemaphoreType.DMA],
        compiler_params=_cparams(("arbitrary",)),
        name="moe_combine",
    )(pos, ys, x, gates, g, b)


def _moe_layer(x, w_hi, w_lo, w1, w3, w2, g, b, alpha, tm, tm_e, tf):
    n = x.shape[0]
    a = n * TOP_K
    e_out, gates = _router(x, w_hi, w_lo, tm)
    e_flat = e_out[:, :TOP_K].reshape(a)
    onehot = (e_flat[:, None] == jnp.arange(N_EXPERTS, dtype=jnp.int32)[None, :]).astype(jnp.int32)
    csum = jnp.cumsum(onehot, axis=0)
    counts = csum[-1]
    rank = jnp.sum((csum - 1) * onehot, axis=1)
    padded = (counts + tm_e - 1) // tm_e * tm_e
    pad_end = jnp.cumsum(padded)
    pad_start = pad_end - padded
    dest = (pad_start[e_flat] + rank).astype(jnp.int32)
    p = -(-(a + N_EXPERTS * tm_e) // GATHER_ROWS) * GATHER_ROWS
    p = -(-p // tm_e) * tm_e
    tok_of_slot = jnp.zeros((p,), jnp.int32).at[dest].set(jnp.arange(a, dtype=jnp.int32) // TOP_K)
    blk_start = jnp.arange(p // tm_e, dtype=jnp.int32) * tm_e
    blk_valid = (blk_start < pad_end[-1]).astype(jnp.int32)
    blk_expert = jnp.minimum(jnp.searchsorted(pad_end, jnp.minimum(blk_start, pad_end[-1] - 1), side="right"),
                             N_EXPERTS - 1).astype(jnp.int32)

    xs = _gather_rows(x, tok_of_slot)
    ys = _expert_ffn(xs, blk_expert, blk_valid, w1, w3, w2, tm_e, tf)
    return _moe_combine(x, ys, dest, gates, g, b, alpha, tm)


def _rope_tables(t_max):
    t = jnp.arange(t_max, dtype=jnp.int32)

    def freqs(pos, dim):
        inv = ROPE_THETA ** (-jnp.arange(0, dim, 2, dtype=F32) / dim)
        ang = pos.astype(F32)[:, None] * inv[None, :]
        return jnp.cos(ang), jnp.sin(ang)

    zeros = lambda w: jnp.zeros((t_max, w), F32)
    c, s = freqs(t, MLA_ROPE)
    half = MLA_ROPE // 2
    m_cos = jnp.concatenate([c, c, zeros(LANES - MLA_ROPE)], axis=1)
    m_sa = jnp.concatenate([-s, zeros(LANES - half)], axis=1)
    m_sb = jnp.concatenate([zeros(half), s, zeros(LANES - MLA_ROPE)], axis=1)
    cr, sr = freqs(t // GRID_W, HEAD_DIM // 2)
    cc, sc = freqs(t % GRID_W, HEAD_DIM // 2)
    q = HEAD_DIM // 4
    head_cos = jnp.concatenate([cr, cr, cc, cc], axis=1)
    head_sa = jnp.concatenate([-sr, zeros(q), -sc, zeros(q)], axis=1)
    head_sb = jnp.concatenate([zeros(q), sr, zeros(q), sc], axis=1)
    rep = lambda x: jnp.concatenate([x] * (LANES // HEAD_DIM), axis=1)
    return jnp.concatenate([m_cos, m_sa, m_sb, rep(head_cos), rep(head_sa), rep(head_sb)], axis=1)


def _chunk_heads(w, axis):
    shape = w.shape
    w = w.reshape(shape[:axis] + (HEADS, HEAD_DIM) + shape[axis + 1:])
    w = jnp.take(w, jnp.array(HEAD_PERM), axis=axis)
    return w.reshape(shape)


def _prep_layer(l, w_in, mla_q_norm, mla_kv_norm, w_uq, w_ukv, ax_q_norm, ax_k_norm,
                w_br_mla, w_br_win, w_br_ax, w_out, ln1_g, ln1_b, ln2_g, ln2_b):
    wi = w_in[l]
    sizes = [MLA_Q_RANK, MLA_KV_RANK, MLA_ROPE, QW, KW, KW, QW, KW, KW, N_BRANCH * D_MODEL]
    offs = [0]
    for s in sizes:
        offs.append(offs[-1] + s)
    cols = [wi[:, offs[j]:offs[j + 1]] for j in range(len(sizes))]
    c_q, c_kv, k_r, wq, wk, wv, aq, ak, av, wg = cols
    w_s = jnp.concatenate([c_q, c_kv, k_r, jnp.zeros((D_MODEL, LANES - MLA_ROPE), F32),
                           _chunk_heads(wq, 1), wk, wv, _chunk_heads(aq, 1), ak, av], axis=1).astype(BF16)
    uq = w_uq[l].reshape(MLA_Q_RANK, MLA_HEADS, MLA_QK)
    uq = jnp.pad(uq, ((0, 0), (0, 0), (0, MLA_QK_PAD - MLA_QK))).reshape(MLA_Q_RANK, MLA_HEADS * MLA_QK_PAD)
    ukv = w_ukv[l].reshape(MLA_KV_RANK, MLA_HEADS, MLA_NOPE + MLA_V)
    ukv = jnp.concatenate([ukv[:, :, :MLA_NOPE].reshape(MLA_KV_RANK, -1),
                           ukv[:, :, MLA_NOPE:].reshape(MLA_KV_RANK, -1)], axis=1)
    head_id = jnp.arange(LANES) // HEAD_DIM
    bd = jnp.where(head_id[:, None] == head_id[None, :], 1.0 / HEAD_DIM, 0.0).astype(BF16)
    row = lambda v: v.reshape(1, -1).astype(F32)
    return dict(
        w_s=w_s, w_g=wg.astype(BF16), w_uq=uq.astype(BF16), w_ukv=ukv.astype(BF16),
        g_q=row(mla_q_norm[l]), g_kv=row(mla_kv_norm[l]),
        g_aq=row(jnp.tile(ax_q_norm[l], LANES // HEAD_DIM)), g_ak=row(jnp.tile(ax_k_norm[l], LANES // HEAD_DIM)),
        bd=bd,
        w_br_mla=w_br_mla[l].astype(BF16),
        w_br_win=_chunk_heads(w_br_win[l], 0).astype(BF16),
        w_br_ax=_chunk_heads(w_br_ax[l], 0).astype(BF16),
        w_out=w_out[l].astype(BF16),
        ln1_g=row(ln1_g[l]), ln1_b=row(ln1_b[l]), ln2_g=row(ln2_g[l]), ln2_b=row(ln2_b[l]),
    )


def kernel(x_prompt, x_sample, emb_ln_g, emb_ln_b, w_in, mla_q_norm, mla_kv_norm, w_uq, w_ukv, win_sink,
           ax_q_norm, ax_k_norm, w_br_mla, w_br_win, w_br_ax, w_out, ln1_g, ln1_b, ln2_g, ln2_b,
           ffn_w1, ffn_w3, ffn_w2, moe_router, moe_w1, moe_w3, moe_w2):
    depth = w_in.shape[0]
    alpha = (2 * depth) ** 0.25
    groups = []
    off = 0
    for xg in (x_prompt, x_sample):
        bsz, seq, _ = xg.shape
        assert off % seq == 0 and seq % GRID_W == 0
        groups.append((off, bsz, seq))
        off += bsz * seq
    n = off
    seq_min = min(g[2] for g in groups)
    seq_max = max(g[2] for g in groups)
    tm = _pick_tile(seq_min, 512)
    tm_e = 512
    x = jnp.concatenate([x_prompt.reshape(-1, D_MODEL), x_sample.reshape(-1, D_MODEL)], axis=0)

    tab = _rope_tables(seq_max)
    pos_blk = jnp.concatenate([jnp.tile(jnp.arange(seq // tm, dtype=jnp.int32), bsz)
                               for (_, bsz, seq) in groups])

    x = _embed_ln(x, emb_ln_g, emb_ln_b, tm)
    for l in range(depth):
        lw = _prep_layer(l, w_in, mla_q_norm, mla_kv_norm, w_uq, w_ukv, ax_q_norm, ax_k_norm,
                         w_br_mla, w_br_win, w_br_ax, w_out, ln1_g, ln1_b, ln2_g, ln2_b)
        qm, km, vm, wq, wk, wv, aq, ak, av = _in_proj(x, pos_blk, lw, tab, tm)
        sink = win_sink[l].astype(F32)
        o_a = jnp.concatenate([_mla_attention(qm, km, vm, *g) for g in groups], axis=0)
        o_b = jnp.concatenate([_window_attention(wq, wk, wv, sink, *g) for g in groups], axis=0)
        o_c = jnp.concatenate([_axial_attention(aq, ak, av, *g) for g in groups], axis=0)
        x = _merge(x, o_a, o_b, o_c, lw, alpha, tm)
        i = l // 2
        if l % 2 == 0:
            d_ff = ffn_w1.shape[2]
            x = _dense_ffn(x, ffn_w1[i].astype(BF16), ffn_w3[i].astype(BF16), ffn_w2[i].astype(BF16),
                           lw["ln2_g"], lw["ln2_b"], alpha, tm, _pick_tile(d_ff, d_ff // 2))
        else:
            router = jnp.pad(moe_router[i], ((0, 0), (0, LANES - N_EXPERTS)))
            r_hi = router.astype(BF16)
            r_lo = (router - r_hi.astype(F32)).astype(BF16)
            d_ffe = moe_w1.shape[3]
            x = _moe_layer(x, r_hi, r_lo, moe_w1[i].astype(BF16), moe_w3[i].astype(BF16),
                           moe_w2[i].astype(BF16), lw["ln2_g"], lw["ln2_b"], alpha, tm, tm_e,
                           _pick_tile(d_ffe, d_ffe // 4))
    n_p = x_prompt.shape[0] * x_prompt.shape[1]
    return (x[:n_p].reshape(x_prompt.shape), x[n_p:].reshape(x_sample.shape))
```

```python
import functools
import math

import jax
import jax.numpy as jnp
from jax import lax
from jax.experimental import pallas as pl
from jax.experimental.pallas import tpu as pltpu

F32 = jnp.float32
BF16 = jnp.bfloat16

D_MODEL = 1024
GRID_W = 64
ROPE_THETA = 10000.0
RMS_EPS = 1e-6
LN_EPS = 1e-5
NEG_INF = -1e30

MLA_HEADS = 4
MLA_Q_RANK = 384
MLA_KV_RANK = 256
MLA_NOPE = 128
MLA_ROPE = 64
MLA_V = 128
MLA_QK = MLA_NOPE + MLA_ROPE

HEADS = 8
KV_HEADS = 2
HEAD_DIM = 64
GROUP = HEADS // KV_HEADS
WINDOW = 128
Q_BLOCK = 128
SPAN = Q_BLOCK + 2 * WINDOW

N_BRANCH = 3
N_EXPERTS = 8
TOP_K = 2

LANES = 128
BF16_ROWS = 16
MXU_DIM = 256
VMEM_LIMIT = 56 * 1024 * 1024

MLA_QK_PAD = MXU_DIM
QW = HEADS * HEAD_DIM
KW = KV_HEADS * HEAD_DIM
C_CQ = 0
C_CKV = C_CQ + MLA_Q_RANK
C_KR = C_CKV + MLA_KV_RANK
C_WQ = C_KR + LANES
C_WK = C_WQ + QW
C_WV = C_WK + KW
C_AQ = C_WV + KW
C_AK = C_AQ + QW
C_AV = C_AK + KW
C_END = C_AV + KW
TAB_W = 6 * LANES
LOG2E = math.log2(math.e)
ATT_TK = 256

HEAD_PERM = [h for c in range(GROUP) for h in (c, GROUP + c)]


def _cparams(sem, vmem=VMEM_LIMIT, flags=None):
    return pltpu.CompilerParams(dimension_semantics=sem, vmem_limit_bytes=vmem, flags=flags)


ATT_FLAGS = None


def _pick_tile(n, pref):
    t = min(pref, n)
    while n % t:
        t //= 2
    return t


def _ln_rows(y, g, b):
    mu = jnp.mean(y, axis=-1, keepdims=True)
    d = y - mu
    var = jnp.mean(d * d, axis=-1, keepdims=True)
    return d * lax.rsqrt(var + LN_EPS) * g + b


def _rms_rows(x, g):
    ms = jnp.mean(x * x, axis=-1, keepdims=True)
    return x * lax.rsqrt(ms + RMS_EPS) * g


def _rope_chunk(x, cos, sin_a, sin_b, half):
    return (x * cos + pltpu.roll(x, LANES - half, 1) * sin_a + pltpu.roll(x, half, 1) * sin_b)


def _dot(a, b):
    return jnp.dot(a, b, preferred_element_type=F32)


def _split_dot(x, w_bf16):
    hi = x.astype(BF16)
    lo = (x - hi.astype(F32)).astype(BF16)
    return _dot(hi, w_bf16) + _dot(lo, w_bf16)


def _ln_kernel(x_ref, g_ref, b_ref, o_ref):
    o_ref[...] = _ln_rows(x_ref[...], g_ref[...], b_ref[...])


def _embed_ln(x, g, b, tm):
    n = x.shape[0]
    return pl.pallas_call(
        _ln_kernel,
        grid=(n // tm,),
        in_specs=[pl.BlockSpec((tm, D_MODEL), lambda i: (i, 0)),
                  pl.BlockSpec((1, D_MODEL), lambda i: (0, 0)),
                  pl.BlockSpec((1, D_MODEL), lambda i: (0, 0))],
        out_specs=pl.BlockSpec((tm, D_MODEL), lambda i: (i, 0)),
        out_shape=jax.ShapeDtypeStruct((n, D_MODEL), F32),
        compiler_params=_cparams(("parallel",)),
        name="embed_ln",
    )(x, g.reshape(1, -1), b.reshape(1, -1))


def _proj_kernel(pos_ref, x_ref, ws_ref, wuq_ref, wukv_ref, gq_ref, gkv_ref, gaq_ref, gak_ref,
                 bd_ref, tab_ref,
                 qm_ref, km_ref, vm_ref, wq_ref, wk_ref, wv_ref, aq_ref, ak_ref, av_ref):
    del pos_ref
    tm = x_ref.shape[0]
    x = x_ref[...].astype(BF16)
    res = _dot(x, ws_ref[...])

    m_cos = tab_ref[:, 0 * LANES:1 * LANES]
    m_sa = tab_ref[:, 1 * LANES:2 * LANES]
    m_sb = tab_ref[:, 2 * LANES:3 * LANES]
    a_cos = tab_ref[:, 3 * LANES:4 * LANES]
    a_sa = tab_ref[:, 4 * LANES:5 * LANES]
    a_sb = tab_ref[:, 5 * LANES:6 * LANES]
    c_q = _rms_rows(res[:, C_CQ:C_CKV], gq_ref[...]).astype(BF16)
    q = _dot(c_q, wuq_ref[...])
    q_scale = MLA_QK ** -0.5 * LOG2E
    for h in range(MLA_HEADS):
        base = h * MLA_QK_PAD
        qm_ref[:, base:base + LANES] = (q[:, base:base + LANES] * q_scale).astype(BF16)
        roped = _rope_chunk(q[:, base + LANES:base + 2 * LANES], m_cos, m_sa, m_sb, MLA_ROPE // 2)
        qm_ref[:, base + LANES:base + 2 * LANES] = (roped * q_scale).astype(BF16)

    c_kv = _rms_rows(res[:, C_CKV:C_KR], gkv_ref[...]).astype(BF16)
    kv = _dot(c_kv, wukv_ref[...])
    k_rope = _rope_chunk(res[:, C_KR:C_WQ], m_cos, m_sa, m_sb, MLA_ROPE // 2).astype(BF16)
    for h in range(MLA_HEADS):
        base = h * MLA_QK_PAD
        km_ref[:, base:base + LANES] = kv[:, h * LANES:(h + 1) * LANES].astype(BF16)
        km_ref[:, base + LANES:base + 2 * LANES] = k_rope
    v_off = MLA_HEADS * MLA_NOPE
    for r in range(tm // ATT_TK):
        rows = slice(r * ATT_TK, (r + 1) * ATT_TK)
        for h in range(MLA_HEADS):
            vm_ref[r, h * MLA_V:(h + 1) * MLA_V, :] = (
                kv[rows, v_off + h * MLA_V:v_off + (h + 1) * MLA_V].T.astype(BF16))
        av_ref[r] = res[rows, C_AV:C_END].T.astype(BF16)

    wq_ref[...] = (res[:, C_WQ:C_WK] * (HEAD_DIM ** -0.5 * LOG2E)).astype(BF16)
    wk_ref[...] = res[:, C_WK:C_WV].astype(BF16)
    for r in range(tm // Q_BLOCK):
        wv_ref[r] = res[r * Q_BLOCK:(r + 1) * Q_BLOCK, C_WV:C_AQ].T.astype(BF16)

    bd = bd_ref[...]

    def norm_rope(xc, g):
        ms = _split_dot(xc * xc, bd)
        xn = xc * lax.rsqrt(ms + RMS_EPS) * g
        return _rope_chunk(xn, a_cos, a_sa, a_sb, HEAD_DIM // 4)

    for c in range(QW // LANES):
        xc = res[:, C_AQ + c * LANES:C_AQ + (c + 1) * LANES]
        aq_ref[:, c * LANES:(c + 1) * LANES] = (
            norm_rope(xc, gaq_ref[...]) * (HEAD_DIM ** -0.5 * LOG2E)).astype(BF16)
    ak_ref[...] = norm_rope(res[:, C_AK:C_AV], gak_ref[...]).astype(BF16)


def _in_proj(x, pos_blk, lw, tab, tm):
    n = x.shape[0]
    const = lambda shape: pl.BlockSpec(shape, lambda i, p: (0, 0))
    rows = lambda w: pl.BlockSpec((tm, w), lambda i, p: (i, 0))
    out_widths = [MLA_HEADS * MLA_QK_PAD, MLA_HEADS * MLA_QK_PAD, (MLA_HEADS * MLA_V, ATT_TK),
                  QW, KW, (KW, Q_BLOCK), QW, KW, (KW, ATT_TK)]

    def out_spec(w):
        if isinstance(w, tuple):
            return pl.BlockSpec((tm // w[1], w[0], w[1]), lambda i, p: (i, 0, 0))
        return rows(w)

    def out_shape(w):
        if isinstance(w, tuple):
            return jax.ShapeDtypeStruct((n // w[1], w[0], w[1]), BF16)
        return jax.ShapeDtypeStruct((n, w), BF16)

    grid_spec = pltpu.PrefetchScalarGridSpec(
        num_scalar_prefetch=1,
        grid=(n // tm,),
        in_specs=[rows(D_MODEL),
                  const((D_MODEL, C_END)),
                  const((MLA_Q_RANK, MLA_HEADS * MLA_QK_PAD)),
                  const((MLA_KV_RANK, 2 * MLA_HEADS * LANES)),
                  const((1, MLA_Q_RANK)), const((1, MLA_KV_RANK)),
                  const((1, LANES)), const((1, LANES)),
                  const((LANES, LANES)),
                  pl.BlockSpec((tm, TAB_W), lambda i, p: (p[i], 0))],
        out_specs=[out_spec(w) for w in out_widths],
    )
    return pl.pallas_call(
        _proj_kernel,
        grid_spec=grid_spec,
        out_shape=[out_shape(w) for w in out_widths],
        compiler_params=_cparams(("parallel",)),
        name="in_proj",
    )(pos_blk, x, lw["w_s"], lw["w_uq"], lw["w_ukv"], lw["g_q"], lw["g_kv"], lw["g_aq"], lw["g_ak"],
      lw["bd"], tab)


def _call_into(prev, kernel, operands, *, in_specs, **kwargs):
    if prev is None:
        prev = jnp.zeros(kwargs["out_shape"].shape, kwargs["out_shape"].dtype)
    n_in = len(operands)

    def skip_prev(*refs):
        return kernel(*refs[:n_in], *refs[n_in + 1:])

    return pl.pallas_call(skip_prev, in_specs=list(in_specs) + [pl.BlockSpec(memory_space=pl.ANY)],
                          input_output_aliases={n_in: 0}, **kwargs)(*operands, prev)


def _flash_t(qt_sc, k_ref, vt_ref, s0_sc, s1_sc, acc_sc, seq, v_rows):
    n_blk = acc_sc.shape[0]
    nk = seq // ATT_TK
    assert nk >= 2 and nk % 2 == 0
    acc_sc[...] = jnp.zeros(acc_sc.shape, F32)
    stat = lambda v: tuple(jnp.full((1, MXU_DIM), v, F32) for _ in range(n_blk))

    def scores(j, s_ref):
        k = k_ref[pl.ds(pl.multiple_of(j * ATT_TK, ATT_TK), ATT_TK), :]
        for b in range(n_blk):
            s_ref[b] = _dot(k, qt_sc[:, b * MXU_DIM:(b + 1) * MXU_DIM])

    ones_rows = jnp.where(lax.broadcasted_iota(jnp.int32, (BF16_ROWS, ATT_TK), 0) == 0, 1.0, 0.0).astype(BF16)

    def consume(j, s_ref, ms):
        ms = list(ms)
        for b in range(n_blk):
            s = s_ref[b]
            m_new = jnp.maximum(ms[b], jnp.max(s, axis=0, keepdims=True))
            alpha = jnp.exp2(ms[b] - m_new)
            p = jnp.exp2((s - m_new).astype(BF16))
            ms[b] = m_new
            r0, r1 = v_rows[b]
            vt = jnp.concatenate([vt_ref[j, r0:r1, :], ones_rows], axis=0)
            acc_sc[b] = acc_sc[b] * alpha + _dot(vt, p)
        return tuple(ms)

    scores(0, s0_sc)

    def pair(i, carry):
        scores(2 * i + 1, s1_sc)
        carry = consume(2 * i, s0_sc, carry)
        scores(2 * i + 2, s0_sc)
        return consume(2 * i + 1, s1_sc, carry)

    carry = lax.fori_loop(0, nk // 2 - 1, pair, stat(NEG_INF), unroll=(nk <= 8))
    scores(nk - 1, s1_sc)
    carry = consume(nk - 2, s0_sc, carry)
    consume(nk - 1, s1_sc, carry)


def _stage_queries(q_ref, qt_sc):
    tq = q_ref.shape[0]
    lo = lax.broadcasted_iota(jnp.int32, (LANES, tq), 0) < HEAD_DIM
    for c in range(GROUP):
        qt = q_ref[:, c * LANES:(c + 1) * LANES].astype(F32).T
        qt_sc[:, c * tq:(c + 1) * tq] = jnp.where(lo, qt, 0.0).astype(BF16)
        qt_sc[:, (GROUP + c) * tq:(GROUP + c + 1) * tq] = jnp.where(lo, 0.0, qt).astype(BF16)


def _store_heads(head_out, o_ref):
    for c in range(GROUP):
        ot = jnp.concatenate([head_out(c), head_out(GROUP + c)], axis=0)
        o_ref[:, c * LANES:(c + 1) * LANES] = ot.T.astype(o_ref.dtype)


def _gqa_kernel(q_ref, k_ref, vt_ref, o_ref, qt_sc, s0_sc, s1_sc, acc_sc, *, seq):
    tq = q_ref.shape[0]
    _stage_queries(q_ref, qt_sc)
    heads_per_blk = MXU_DIM // tq
    v_rows = [((b * heads_per_blk) // GROUP * HEAD_DIM, ((b * heads_per_blk) // GROUP + 1) * HEAD_DIM)
              for b in range(acc_sc.shape[0])]
    _flash_t(qt_sc, k_ref, vt_ref, s0_sc, s1_sc, acc_sc, seq, v_rows)

    def head_out(h):
        b, off = divmod(h * tq, MXU_DIM)
        return acc_sc[b, 0:HEAD_DIM, off:off + tq] / acc_sc[b, HEAD_DIM:HEAD_DIM + 1, off:off + tq]

    _store_heads(head_out, o_ref)


def _axial_attention(q, k, vt, prev, tok_off, batch, seq):
    tq = LANES
    nq = seq // tq
    nkb = seq // ATT_TK
    qb, sb = tok_off // tq, tok_off // seq
    n_blk = HEADS * tq // MXU_DIM
    return _call_into(
        prev, functools.partial(_gqa_kernel, seq=seq), (q, k, vt),
        grid=(batch, nq),
        in_specs=[pl.BlockSpec((tq, QW), lambda b, i: (qb + b * nq + i, 0)),
                  pl.BlockSpec((seq, KW), lambda b, i: (sb + b, 0)),
                  pl.BlockSpec((nkb, KW, ATT_TK), lambda b, i: (sb + b, 0, 0))],
        out_specs=pl.BlockSpec((tq, QW), lambda b, i: (qb + b * nq + i, 0)),
        out_shape=jax.ShapeDtypeStruct((q.shape[0], QW), BF16),
        scratch_shapes=[pltpu.VMEM((KW, HEADS * tq), BF16),
                        pltpu.VMEM((n_blk, ATT_TK, MXU_DIM), F32),
                        pltpu.VMEM((n_blk, ATT_TK, MXU_DIM), F32),
                        pltpu.VMEM((n_blk, HEAD_DIM + BF16_ROWS, MXU_DIM), F32)],
        compiler_params=_cparams(("parallel", "arbitrary"), flags=ATT_FLAGS),
        name="axial_attention",
    )


def _mla_kernel(q_ref, k_ref, vt_ref, o_ref, qt_sc, s0_sc, s1_sc, acc_sc, *, seq):
    n_blk = acc_sc.shape[0]
    for b in range(n_blk):
        qt_sc[:, b * MXU_DIM:(b + 1) * MXU_DIM] = (
            q_ref[b * MXU_DIM:(b + 1) * MXU_DIM, :].astype(F32).T.astype(BF16))
    _flash_t(qt_sc, k_ref, vt_ref, s0_sc, s1_sc, acc_sc, seq, [(0, MLA_V)] * n_blk)
    for b in range(n_blk):
        o = acc_sc[b, 0:MLA_V, :] / acc_sc[b, MLA_V:MLA_V + 1, :]
        o_ref[b * MXU_DIM:(b + 1) * MXU_DIM, :] = o.T.astype(o_ref.dtype)


def _mla_attention(q, k, vt, prev, tok_off, batch, seq):
    tq = _pick_tile(seq, 4 * MXU_DIM)
    assert tq % MXU_DIM == 0
    nq = seq // tq
    nkb = seq // ATT_TK
    qb, sb = tok_off // tq, tok_off // seq
    return _call_into(
        prev, functools.partial(_mla_kernel, seq=seq), (q, k, vt),
        grid=(batch, MLA_HEADS, nq),
        in_specs=[pl.BlockSpec((tq, MLA_QK_PAD), lambda b, h, i: (qb + b * nq + i, h)),
                  pl.BlockSpec((seq, MLA_QK_PAD), lambda b, h, i: (sb + b, h)),
                  pl.BlockSpec((nkb, MLA_V, ATT_TK), lambda b, h, i: (sb + b, h, 0))],
        out_specs=pl.BlockSpec((tq, MLA_V), lambda b, h, i: (qb + b * nq + i, h)),
        out_shape=jax.ShapeDtypeStruct((q.shape[0], MLA_HEADS * MLA_V), BF16),
        scratch_shapes=[pltpu.VMEM((MLA_QK_PAD, tq), BF16),
                        pltpu.VMEM((tq // MXU_DIM, ATT_TK, MXU_DIM), F32),
                        pltpu.VMEM((tq // MXU_DIM, ATT_TK, MXU_DIM), F32),
                        pltpu.VMEM((tq // MXU_DIM, MLA_V + BF16_ROWS, MXU_DIM), F32)],
        compiler_params=_cparams(("parallel", "parallel", "arbitrary"), flags=ATT_FLAGS),
        name="mla_attention",
    )


def _window_kernel(sink_ref, q_ref, k_ref, vt_ref, o_ref, qt_sc, *, seq):
    i = pl.program_id(1)
    start = i * Q_BLOCK
    kstart = pl.multiple_of(jnp.clip(start - WINDOW, 0, seq - SPAN), Q_BLOCK)
    kb = kstart // Q_BLOCK
    _stage_queries(q_ref, qt_sc)
    s_all = _dot(k_ref[pl.ds(kstart, SPAN), :], qt_sc[...])
    vt = jnp.concatenate([vt_ref[kb + j] for j in range(SPAN // Q_BLOCK)], axis=1)
    ones_rows = jnp.where(lax.broadcasted_iota(jnp.int32, (BF16_ROWS, SPAN), 0) == 0, 1.0, 0.0).astype(BF16)
    rel = ((kstart + lax.broadcasted_iota(jnp.int32, (SPAN, Q_BLOCK), 0))
           - (start + lax.broadcasted_iota(jnp.int32, (SPAN, Q_BLOCK), 1)))
    dist_i = jnp.abs(rel)
    dist = jnp.where(dist_i <= WINDOW, dist_i.astype(F32), -NEG_INF * 2.0 ** HEADS)
    outs = []
    for blk in range(HEADS // 2):
        ps, sinks = [], []
        for h in (2 * blk, 2 * blk + 1):
            slope = 2.0 ** (-8.0 * (h + 1) / HEADS) * LOG2E
            sink = sink_ref[h] * LOG2E
            logits = s_all[:, h * Q_BLOCK:(h + 1) * Q_BLOCK] - slope * dist
            m = jnp.maximum(jnp.max(logits, axis=0, keepdims=True), sink)
            ps.append(jnp.exp2((logits - m).astype(BF16)))
            sinks.append(jnp.exp2(sink - m))
        g = (2 * blk) // GROUP
        lhs = jnp.concatenate([vt[g * HEAD_DIM:(g + 1) * HEAD_DIM], ones_rows], axis=0)
        pv = _dot(lhs, jnp.concatenate(ps, axis=1))
        denom = pv[HEAD_DIM:HEAD_DIM + 1] + jnp.concatenate(sinks, axis=1)
        outs.append(pv[0:HEAD_DIM] / denom)
    _store_heads(lambda h: outs[h // 2][:, (h % 2) * Q_BLOCK:(h % 2 + 1) * Q_BLOCK], o_ref)


def _window_attention(q, k, v, sink, prev, tok_off, batch, seq):
    assert seq >= SPAN and seq % Q_BLOCK == 0
    nq = seq // Q_BLOCK
    qb, sb = tok_off // Q_BLOCK, tok_off // seq
    return _call_into(
        prev, functools.partial(_window_kernel, seq=seq), (sink, q, k, v),
        grid=(batch, nq),
        in_specs=[pl.BlockSpec(memory_space=pltpu.SMEM),
                  pl.BlockSpec((Q_BLOCK, QW), lambda b, i: (qb + b * nq + i, 0)),
                  pl.BlockSpec((seq, KW), lambda b, i: (sb + b, 0)),
                  pl.BlockSpec((nq, KW, Q_BLOCK), lambda b, i: (sb + b, 0, 0))],
        out_specs=pl.BlockSpec((Q_BLOCK, QW), lambda b, i: (qb + b * nq + i, 0)),
        out_shape=jax.ShapeDtypeStruct((q.shape[0], QW), BF16),
        scratch_shapes=[pltpu.VMEM((KW, HEADS * Q_BLOCK), BF16)],
        compiler_params=_cparams(("parallel", "arbitrary")),
        name="window_attention",
    )


def _merge_kernel(x_ref, oa_ref, ob_ref, oc_ref, wg_ref, wa_ref, wb_ref, wc_ref, wo_ref, g_ref, b_ref,
                  o_ref, *, alpha):
    x = x_ref[...]
    xb = x.astype(BF16)
    merged = None
    for idx, (o_br, w_br) in enumerate(((oa_ref, wa_ref), (ob_ref, wb_ref), (oc_ref, wc_ref))):
        gate = jax.nn.sigmoid(_dot(xb, wg_ref[:, idx * D_MODEL:(idx + 1) * D_MODEL]))
        term = gate * _dot(o_br[...], w_br[...])
        merged = term if merged is None else merged + term
    m = _dot(merged.astype(BF16), wo_ref[...])
    o_ref[...] = _ln_rows(alpha * x + m, g_ref[...], b_ref[...])


def _merge(x, o_a, o_b, o_c, lw, alpha, tm):
    n = x.shape[0]
    const = lambda shape: pl.BlockSpec(shape, lambda i: (0, 0))
    rows = lambda w: pl.BlockSpec((tm, w), lambda i: (i, 0))
    return pl.pallas_call(
        functools.partial(_merge_kernel, alpha=alpha),
        grid=(n // tm,),
        in_specs=[rows(D_MODEL), rows(MLA_HEADS * MLA_V), rows(QW), rows(QW),
                  const((D_MODEL, N_BRANCH * D_MODEL)),
                  const((MLA_HEADS * MLA_V, D_MODEL)), const((QW, D_MODEL)), const((QW, D_MODEL)),
                  const((D_MODEL, D_MODEL)), const((1, D_MODEL)), const((1, D_MODEL))],
        out_specs=rows(D_MODEL),
        out_shape=jax.ShapeDtypeStruct((n, D_MODEL), F32),
        compiler_params=_cparams(("parallel",)),
        name="gated_merge",
    )(x, o_a, o_b, o_c, lw["w_g"], lw["w_br_mla"], lw["w_br_win"], lw["w_br_ax"], lw["w_out"],
      lw["ln1_g"], lw["ln1_b"])


def _ffn_kernel(x_ref, w1_ref, w3_ref, w2_ref, g_ref, b_ref, o_ref, acc_ref, *, alpha):
    f = pl.program_id(1)

    @pl.when(f == 0)
    def _():
        acc_ref[...] = jnp.zeros(acc_ref.shape, F32)

    xb = x_ref[...].astype(BF16)
    h = jax.nn.silu(_dot(xb, w1_ref[...])) * _dot(xb, w3_ref[...])
    acc_ref[...] += _dot(h.astype(BF16), w2_ref[...])

    @pl.when(f == pl.num_programs(1) - 1)
    def _():
        o_ref[...] = _ln_rows(alpha * x_ref[...] + acc_ref[...], g_ref[...], b_ref[...])


def _dense_ffn(x, w1, w3, w2, g, b, alpha, tm, tf):
    n = x.shape[0]
    d_ff = w1.shape[1]
    mode = dict(pipeline_mode=pl.Buffered(1)) if tf == d_ff else {}
    return pl.pallas_call(
        functools.partial(_ffn_kernel, alpha=alpha),
        grid=(n // tm, d_ff // tf),
        in_specs=[pl.BlockSpec((tm, D_MODEL), lambda i, f: (i, 0)),
                  pl.BlockSpec((D_MODEL, tf), lambda i, f: (0, f), **mode),
                  pl.BlockSpec((D_MODEL, tf), lambda i, f: (0, f), **mode),
                  pl.BlockSpec((tf, D_MODEL), lambda i, f: (f, 0), **mode),
                  pl.BlockSpec((1, D_MODEL), lambda i, f: (0, 0)),
                  pl.BlockSpec((1, D_MODEL), lambda i, f: (0, 0))],
        out_specs=pl.BlockSpec((tm, D_MODEL), lambda i, f: (i, 0)),
        out_shape=jax.ShapeDtypeStruct((n, D_MODEL), F32),
        scratch_shapes=[pltpu.VMEM((tm, D_MODEL), F32)],
        compiler_params=_cparams(("parallel", "arbitrary")),
        name="dense_ffn",
    )(x, w1, w3, w2, g, b)


def _router_kernel(x_ref, whi_ref, wlo_ref, e_ref, g_ref):
    x = x_ref[...]
    hi = x.astype(BF16)
    lo = (x - hi.astype(F32)).astype(BF16)
    logits = _dot(hi, whi_ref[...]) + _dot(lo, whi_ref[...]) + _dot(hi, wlo_ref[...])
    lane = lax.broadcasted_iota(jnp.int32, logits.shape, 1)
    logits = jnp.where(lane < N_EXPERTS, logits, NEG_INF)
    t1 = jnp.max(logits, axis=1, keepdims=True)
    e1 = jnp.min(jnp.where(logits == t1, lane, LANES), axis=1, keepdims=True)
    rest = jnp.where(lane == e1, NEG_INF, logits)
    t2 = jnp.max(rest, axis=1, keepdims=True)
    e2 = jnp.min(jnp.where(rest == t2, lane, LANES), axis=1, keepdims=True)
    w = jnp.exp(t2 - t1)
    g1 = 1.0 / (1.0 + w)
    g2 = w / (1.0 + w)
    e_ref[...] = jnp.where(lane == 0, e1, jnp.where(lane == 1, e2, 0))
    g_ref[...] = jnp.where(lane == 0, g1, jnp.where(lane == 1, g2, 0.0))


def _router(x, w_hi, w_lo, tm):
    n = x.shape[0]
    return pl.pallas_call(
        _router_kernel,
        grid=(n // tm,),
        in_specs=[pl.BlockSpec((tm, D_MODEL), lambda i: (i, 0)),
                  pl.BlockSpec((D_MODEL, LANES), lambda i: (0, 0)),
                  pl.BlockSpec((D_MODEL, LANES), lambda i: (0, 0))],
        out_specs=[pl.BlockSpec((tm, LANES), lambda i: (i, 0)),
                   pl.BlockSpec((tm, LANES), lambda i: (i, 0))],
        out_shape=[jax.ShapeDtypeStruct((n, LANES), jnp.int32),
                   jax.ShapeDtypeStruct((n, LANES), F32)],
        compiler_params=_cparams(("parallel",)),
        name="moe_router",
    )(x, w_hi, w_lo)


ROW_TILE = 8


def _row_of(ref3, r):
    return ref3.at[lax.shift_right_logical(r, 3), pl.ds(jnp.bitwise_and(r, ROW_TILE - 1), 1)]


def _dispatch_kernel(pad_ref, dest_hbm, x_ref, xs_hbm, dest_smem, zero_ref, idx_sem, row_sem, *, n_pad, zpad):
    i = pl.program_id(0)
    groups = x_ref.shape[0]
    n_idx = TOP_K * ROW_TILE * groups
    cp = pltpu.make_async_copy(dest_hbm.at[pl.ds(pl.multiple_of(i * n_idx, n_idx), n_idx)], dest_smem, idx_sem)
    cp.start()
    zero_ref[...] = jnp.zeros(zero_ref.shape, F32)
    cp.wait()

    def send(g, carry):
        for u in range(ROW_TILE):
            for k in range(TOP_K):
                d = dest_smem[(g * ROW_TILE + u) * TOP_K + k]
                pltpu.make_async_copy(x_ref.at[g, pl.ds(u, 1)], _row_of(xs_hbm, d), row_sem).start(priority=k)
        return carry

    lax.fori_loop(0, groups, send, 0)

    def pad_copy(r):
        return pltpu.make_async_copy(zero_ref.at[pl.ds(0, 1)], _row_of(xs_hbm, pad_ref[i * zpad + r]), row_sem)

    def send_zero(r, carry):
        @pl.when(i * zpad + r < n_pad)
        def _():
            pad_copy(r).start()
        return carry

    def drain_zero(r, carry):
        @pl.when(i * zpad + r < n_pad)
        def _():
            pad_copy(r).wait()
        return carry

    def drain(r, carry):
        pltpu.make_async_copy(x_ref.at[0, pl.ds(0, 1)], xs_hbm.at[0, pl.ds(0, 1)], row_sem).wait()
        return carry

    lax.fori_loop(0, zpad, send_zero, 0)
    lax.fori_loop(0, n_idx, drain, 0, unroll=8)
    lax.fori_loop(0, zpad, drain_zero, 0)


def _dispatch_rows(x, dest, pad_slots, p, tm):
    n = x.shape[0]
    steps = n // tm
    n_pad = pad_slots.shape[0]
    zpad = -(-n_pad // steps)
    grid_spec = pltpu.PrefetchScalarGridSpec(
        num_scalar_prefetch=1,
        grid=(steps,),
        in_specs=[pl.BlockSpec(memory_space=pl.ANY),
                  pl.BlockSpec((tm // ROW_TILE, ROW_TILE, D_MODEL), lambda i, pad: (i, 0, 0))],
        out_specs=pl.BlockSpec(memory_space=pl.ANY),
        scratch_shapes=[pltpu.SMEM((TOP_K * tm,), jnp.int32),
                        pltpu.VMEM((ROW_TILE, D_MODEL), F32),
                        pltpu.SemaphoreType.DMA, pltpu.SemaphoreType.DMA],
    )
    xs = pl.pallas_call(
        functools.partial(_dispatch_kernel, n_pad=n_pad, zpad=zpad),
        grid_spec=grid_spec,
        out_shape=jax.ShapeDtypeStruct((p // ROW_TILE, ROW_TILE, D_MODEL), F32),
        compiler_params=_cparams(("arbitrary",)),
        name="moe_dispatch",
    )(pad_slots, dest, x.reshape(n // ROW_TILE, ROW_TILE, D_MODEL))
    return xs.reshape(p, D_MODEL)


def _expert_kernel(be_ref, bv_ref, x_ref, w1_ref, w3_ref, w2_ref, o_ref, acc_ref):
    del be_ref
    i = pl.program_id(0)
    f = pl.program_id(1)
    valid = bv_ref[i] != 0

    @pl.when(f == 0)
    def _():
        acc_ref[...] = jnp.zeros(acc_ref.shape, F32)

    @pl.when(valid)
    def _():
        xb = x_ref[...].astype(BF16)
        h = jax.nn.silu(_dot(xb, w1_ref[0])) * _dot(xb, w3_ref[0])
        acc_ref[...] += _dot(h.astype(BF16), w2_ref[0])

    @pl.when(f == pl.num_programs(1) - 1)
    def _():
        o_ref[...] = acc_ref[...]


def _expert_ffn(xs, blk_expert, blk_valid, w1, w3, w2, tm, tf):
    p = xs.shape[0]
    d_ff = w1.shape[2]
    nf = d_ff // tf
    fidx = lambda i, f, be, bv: jnp.where(bv[i] != 0, f, nf - 1)
    grid_spec = pltpu.PrefetchScalarGridSpec(
        num_scalar_prefetch=2,
        grid=(p // tm, nf),
        in_specs=[pl.BlockSpec((tm, D_MODEL), lambda i, f, be, bv: (i, 0)),
                  pl.BlockSpec((1, D_MODEL, tf), lambda i, f, be, bv: (be[i], 0, fidx(i, f, be, bv))),
                  pl.BlockSpec((1, D_MODEL, tf), lambda i, f, be, bv: (be[i], 0, fidx(i, f, be, bv))),
                  pl.BlockSpec((1, tf, D_MODEL), lambda i, f, be, bv: (be[i], fidx(i, f, be, bv), 0))],
        out_specs=pl.BlockSpec((tm, D_MODEL), lambda i, f, be, bv: (i, 0)),
        scratch_shapes=[pltpu.VMEM((tm, D_MODEL), F32)],
    )
    return pl.pallas_call(
        _expert_kernel,
        grid_spec=grid_spec,
        out_shape=jax.ShapeDtypeStruct((p, D_MODEL), F32),
        compiler_params=_cparams(("arbitrary", "arbitrary")),
        name="moe_experts",
    )(blk_expert, blk_valid, xs, w1, w3, w2)


def _combine_kernel(pos_hbm, ys_hbm, x_ref, gate_ref, g_ref, b_ref, o_ref,
                    r0_ref, r1_ref, pos_smem, pos_sem, row_sem, *, alpha):
    i = pl.program_id(0)
    tm = x_ref.shape[0]
    n_idx = TOP_K * tm
    cp = pltpu.make_async_copy(pos_hbm.at[pl.ds(pl.multiple_of(i * n_idx, n_idx), n_idx)], pos_smem, pos_sem)
    cp.start()
    cp.wait()

    def fetch(g, carry):
        for u in range(ROW_TILE):
            for k, buf in enumerate((r0_ref, r1_ref)):
                src = _row_of(ys_hbm, pos_smem[(g * ROW_TILE + u) * TOP_K + k])
                pltpu.make_async_copy(src, buf.at[g, pl.ds(u, 1)], row_sem).start(priority=k)
        return carry

    def drain(r, carry):
        pltpu.make_async_copy(ys_hbm.at[0, pl.ds(0, 1)], r0_ref.at[0, pl.ds(0, 1)], row_sem).wait()
        return carry

    lax.fori_loop(0, tm // ROW_TILE, fetch, 0)
    lax.fori_loop(0, n_idx, drain, 0, unroll=8)
    gates = gate_ref[...]
    y = (gates[:, 0:1] * r0_ref[...].reshape(tm, D_MODEL) + gates[:, 1:2] * r1_ref[...].reshape(tm, D_MODEL))
    o_ref[...] = _ln_rows(alpha * x_ref[...] + y, g_ref[...], b_ref[...])


def _moe_combine(x, ys, pos, gates, g, b, alpha, tm):
    n = x.shape[0]
    return pl.pallas_call(
        functools.partial(_combine_kernel, alpha=alpha),
        grid=(n // tm,),
        in_specs=[pl.BlockSpec(memory_space=pl.ANY), pl.BlockSpec(memory_space=pl.ANY),
                  pl.BlockSpec((tm, D_MODEL), lambda i: (i, 0)),
                  pl.BlockSpec((tm, LANES), lambda i: (i, 0)),
                  pl.BlockSpec((1, D_MODEL), lambda i: (0, 0)),
                  pl.BlockSpec((1, D_MODEL), lambda i: (0, 0))],
        out_specs=pl.BlockSpec((tm, D_MODEL), lambda i: (i, 0)),
        out_shape=jax.ShapeDtypeStruct((n, D_MODEL), F32),
        scratch_shapes=[pltpu.VMEM((tm // ROW_TILE, ROW_TILE, D_MODEL), F32),
                        pltpu.VMEM((tm // ROW_TILE, ROW_TILE, D_MODEL), F32),
                        pltpu.SMEM((TOP_K * tm,), jnp.int32),
                        pltpu.SemaphoreType.DMA, pltpu.SemaphoreType.DMA],
        compiler_params=_cparams(("arbitrary",)),
        name="moe_combine",
    )(pos, ys.reshape(ys.shape[0] // ROW_TILE, ROW_TILE, D_MODEL), x, gates, g, b)


def _moe_layer(x, w_hi, w_lo, w1, w3, w2, g, b, alpha, tm, tm_e, tf):
    n = x.shape[0]
    a = n * TOP_K
    e_out, gates = _router(x, w_hi, w_lo, tm)
    e_flat = e_out[:, :TOP_K].reshape(a)
    onehot = (e_flat[:, None] == jnp.arange(N_EXPERTS, dtype=jnp.int32)[None, :]).astype(jnp.int32)
    csum = jnp.cumsum(onehot, axis=0)
    counts = csum[-1]
    rank = jnp.sum((csum - 1) * onehot, axis=1)
    padded = (counts + tm_e - 1) // tm_e * tm_e
    pad_end = jnp.cumsum(padded)
    pad_start = pad_end - padded
    dest = (pad_start[e_flat] + rank).astype(jnp.int32)
    assert a % tm_e == 0
    n_pad = N_EXPERTS * tm_e
    p = a + n_pad
    npad = padded - counts
    pad_cum = jnp.cumsum(npad)
    pad_i = jnp.arange(n_pad, dtype=jnp.int32)
    pad_e = jnp.searchsorted(pad_cum, pad_i, side="right")
    pad_ec = jnp.minimum(pad_e, N_EXPERTS - 1)
    pad_slots = jnp.where(pad_e >= N_EXPERTS, pad_end[-1] + (pad_i - pad_cum[-1]),
                          (pad_start + counts)[pad_ec] + (pad_i - (pad_cum - npad)[pad_ec])).astype(jnp.int32)
    blk_start = jnp.arange(p // tm_e, dtype=jnp.int32) * tm_e
    blk_valid = (blk_start < pad_end[-1]).astype(jnp.int32)
    blk_expert = jnp.minimum(jnp.searchsorted(pad_end, jnp.minimum(blk_start, pad_end[-1] - 1), side="right"),
                             N_EXPERTS - 1).astype(jnp.int32)

    xs = _dispatch_rows(x, dest, pad_slots, p, tm)
    ys = _expert_ffn(xs, blk_expert, blk_valid, w1, w3, w2, tm_e, tf)
    return _moe_combine(x, ys, dest, gates, g, b, alpha, tm)


def _rope_tables(t_max):
    t = jnp.arange(t_max, dtype=jnp.int32)

    def freqs(pos, dim):
        inv = ROPE_THETA ** (-jnp.arange(0, dim, 2, dtype=F32) / dim)
        ang = pos.astype(F32)[:, None] * inv[None, :]
        return jnp.cos(ang), jnp.sin(ang)

    zeros = lambda w: jnp.zeros((t_max, w), F32)
    c, s = freqs(t, MLA_ROPE)
    half = MLA_ROPE // 2
    m_cos = jnp.concatenate([c, c, zeros(LANES - MLA_ROPE)], axis=1)
    m_sa = jnp.concatenate([-s, zeros(LANES - half)], axis=1)
    m_sb = jnp.concatenate([zeros(half), s, zeros(LANES - MLA_ROPE)], axis=1)
    cr, sr = freqs(t // GRID_W, HEAD_DIM // 2)
    cc, sc = freqs(t % GRID_W, HEAD_DIM // 2)
    q = HEAD_DIM // 4
    head_cos = jnp.concatenate([cr, cr, cc, cc], axis=1)
    head_sa = jnp.concatenate([-sr, zeros(q), -sc, zeros(q)], axis=1)
    head_sb = jnp.concatenate([zeros(q), sr, zeros(q), sc], axis=1)
    rep = lambda x: jnp.concatenate([x] * (LANES // HEAD_DIM), axis=1)
    return jnp.concatenate([m_cos, m_sa, m_sb, rep(head_cos), rep(head_sa), rep(head_sb)], axis=1)


def _chunk_heads(w, axis):
    shape = w.shape
    w = w.reshape(shape[:axis] + (HEADS, HEAD_DIM) + shape[axis + 1:])
    w = jnp.take(w, jnp.array(HEAD_PERM), axis=axis)
    return w.reshape(shape)


def _prep_layer(l, w_in, mla_q_norm, mla_kv_norm, w_uq, w_ukv, ax_q_norm, ax_k_norm,
                w_br_mla, w_br_win, w_br_ax, w_out, ln1_g, ln1_b, ln2_g, ln2_b):
    wi = w_in[l]
    sizes = [MLA_Q_RANK, MLA_KV_RANK, MLA_ROPE, QW, KW, KW, QW, KW, KW, N_BRANCH * D_MODEL]
    offs = [0]
    for s in sizes:
        offs.append(offs[-1] + s)
    cols = [wi[:, offs[j]:offs[j + 1]] for j in range(len(sizes))]
    c_q, c_kv, k_r, wq, wk, wv, aq, ak, av, wg = cols
    w_s = jnp.concatenate([c_q, c_kv, k_r, jnp.zeros((D_MODEL, LANES - MLA_ROPE), F32),
                           _chunk_heads(wq, 1), wk, wv, _chunk_heads(aq, 1), ak, av], axis=1).astype(BF16)
    uq = w_uq[l].reshape(MLA_Q_RANK, MLA_HEADS, MLA_QK)
    uq = jnp.pad(uq, ((0, 0), (0, 0), (0, MLA_QK_PAD - MLA_QK))).reshape(MLA_Q_RANK, MLA_HEADS * MLA_QK_PAD)
    ukv = w_ukv[l].reshape(MLA_KV_RANK, MLA_HEADS, MLA_NOPE + MLA_V)
    ukv = jnp.concatenate([ukv[:, :, :MLA_NOPE].reshape(MLA_KV_RANK, -1),
                           ukv[:, :, MLA_NOPE:].reshape(MLA_KV_RANK, -1)], axis=1)
    head_id = jnp.arange(LANES) // HEAD_DIM
    bd = jnp.where(head_id[:, None] == head_id[None, :], 1.0 / HEAD_DIM, 0.0).astype(BF16)
    row = lambda v: v.reshape(1, -1).astype(F32)
    return dict(
        w_s=w_s, w_g=wg.astype(BF16), w_uq=uq.astype(BF16), w_ukv=ukv.astype(BF16),
        g_q=row(mla_q_norm[l]), g_kv=row(mla_kv_norm[l]),
        g_aq=row(jnp.tile(ax_q_norm[l], LANES // HEAD_DIM)), g_ak=row(jnp.tile(ax_k_norm[l], LANES // HEAD_DIM)),
        bd=bd,
        w_br_mla=w_br_mla[l].astype(BF16),
        w_br_win=_chunk_heads(w_br_win[l], 0).astype(BF16),
        w_br_ax=_chunk_heads(w_br_ax[l], 0).astype(BF16),
        w_out=w_out[l].astype(BF16),
        ln1_g=row(ln1_g[l]), ln1_b=row(ln1_b[l]), ln2_g=row(ln2_g[l]), ln2_b=row(ln2_b[l]),
    )


def kernel(x_prompt, x_sample, emb_ln_g, emb_ln_b, w_in, mla_q_norm, mla_kv_norm, w_uq, w_ukv, win_sink,
           ax_q_norm, ax_k_norm, w_br_mla, w_br_win, w_br_ax, w_out, ln1_g, ln1_b, ln2_g, ln2_b,
           ffn_w1, ffn_w3, ffn_w2, moe_router, moe_w1, moe_w3, moe_w2):
    depth = w_in.shape[0]
    alpha = (2 * depth) ** 0.25
    groups = []
    off = 0
    for xg in (x_prompt, x_sample):
        bsz, seq, _ = xg.shape
        assert off % seq == 0 and seq % GRID_W == 0
        groups.append((off, bsz, seq))
        off += bsz * seq
    n = off
    seq_min = min(g[2] for g in groups)
    seq_max = max(g[2] for g in groups)
    tm = _pick_tile(seq_min, 512)
    tm_e = 512
    x = jnp.concatenate([x_prompt.reshape(-1, D_MODEL), x_sample.reshape(-1, D_MODEL)], axis=0)

    tab = _rope_tables(seq_max)
    pos_blk = jnp.concatenate([jnp.tile(jnp.arange(seq // tm, dtype=jnp.int32), bsz)
                               for (_, bsz, seq) in groups])

    x = _embed_ln(x, emb_ln_g, emb_ln_b, tm)
    for l in range(depth):
        lw = _prep_layer(l, w_in, mla_q_norm, mla_kv_norm, w_uq, w_ukv, ax_q_norm, ax_k_norm,
                         w_br_mla, w_br_win, w_br_ax, w_out, ln1_g, ln1_b, ln2_g, ln2_b)
        qm, km, vm, wq, wk, wv, aq, ak, av = _in_proj(x, pos_blk, lw, tab, tm)
        sink = win_sink[l].astype(F32)
        o_a = o_b = o_c = None
        for g in groups:
            o_a = _mla_attention(qm, km, vm, o_a, *g)
            o_b = _window_attention(wq, wk, wv, sink, o_b, *g)
            o_c = _axial_attention(aq, ak, av, o_c, *g)
        x = _merge(x, o_a, o_b, o_c, lw, alpha, tm)
        i = l // 2
        if l % 2 == 0:
            d_ff = ffn_w1.shape[2]
            x = _dense_ffn(x, ffn_w1[i].astype(BF16), ffn_w3[i].astype(BF16), ffn_w2[i].astype(BF16),
                           lw["ln2_g"], lw["ln2_b"], alpha, tm, d_ff)
        else:
            router = jnp.pad(moe_router[i], ((0, 0), (0, LANES - N_EXPERTS)))
            r_hi = router.astype(BF16)
            r_lo = (router - r_hi.astype(F32)).astype(BF16)
            d_ffe = moe_w1.shape[3]
            x = _moe_layer(x, r_hi, r_lo, moe_w1[i].astype(BF16), moe_w3[i].astype(BF16),
                           moe_w2[i].astype(BF16), lw["ln2_g"], lw["ln2_b"], alpha, tm, tm_e,
                           d_ffe // 2)
    n_p = x_prompt.shape[0] * x_prompt.shape[1]
    return (x[:n_p].reshape(x_prompt.shape), x[n_p:].reshape(x_sample.shape))
```

```python
import functools
import math

import jax
import jax.numpy as jnp
from jax import lax
from jax.experimental import pallas as pl
from jax.experimental.pallas import tpu as pltpu

F32 = jnp.float32
BF16 = jnp.bfloat16

D_MODEL = 1024
GRID_W = 64
ROPE_THETA = 10000.0
RMS_EPS = 1e-6
LN_EPS = 1e-5
NEG_INF = -1e30

MLA_HEADS = 4
MLA_Q_RANK = 384
MLA_KV_RANK = 256
MLA_NOPE = 128
MLA_ROPE = 64
MLA_V = 128
MLA_QK = MLA_NOPE + MLA_ROPE

HEADS = 8
KV_HEADS = 2
HEAD_DIM = 64
GROUP = HEADS // KV_HEADS
WINDOW = 128
Q_BLOCK = 128
SPAN = Q_BLOCK + 2 * WINDOW

N_BRANCH = 3
N_EXPERTS = 8
TOP_K = 2

LANES = 128
BF16_ROWS = 16
MXU_DIM = 256
VMEM_LIMIT = 56 * 1024 * 1024

MLA_QK_PAD = MXU_DIM
QW = HEADS * HEAD_DIM
KW = KV_HEADS * HEAD_DIM
C_CQ = 0
C_CKV = C_CQ + MLA_Q_RANK
C_KR = C_CKV + MLA_KV_RANK
C_WQ = C_KR + LANES
C_WK = C_WQ + QW
C_WV = C_WK + KW
C_AQ = C_WV + KW
C_AK = C_AQ + QW
C_AV = C_AK + KW
C_END = C_AV + KW
TAB_W = 6 * LANES
LOG2E = math.log2(math.e)
ATT_TK = 256

HEAD_PERM = [h for c in range(GROUP) for h in (c, GROUP + c)]


def _cparams(sem, vmem=VMEM_LIMIT, flags=None):
    return pltpu.CompilerParams(dimension_semantics=sem, vmem_limit_bytes=vmem, flags=flags)


ATT_FLAGS = None


def _pick_tile(n, pref):
    t = min(pref, n)
    while n % t:
        t //= 2
    return t


def _ln_rows(y, g, b):
    mu = jnp.mean(y, axis=-1, keepdims=True)
    d = y - mu
    var = jnp.mean(d * d, axis=-1, keepdims=True)
    return d * lax.rsqrt(var + LN_EPS) * g + b


def _rms_rows(x, g):
    ms = jnp.mean(x * x, axis=-1, keepdims=True)
    return x * lax.rsqrt(ms + RMS_EPS) * g


def _rope_chunk(x, cos, sin_a, sin_b, half):
    return (x * cos + pltpu.roll(x, LANES - half, 1) * sin_a + pltpu.roll(x, half, 1) * sin_b)


def _dot(a, b):
    return jnp.dot(a, b, preferred_element_type=F32)


def _split_dot(x, w_bf16):
    hi = x.astype(BF16)
    lo = (x - hi.astype(F32)).astype(BF16)
    return _dot(hi, w_bf16) + _dot(lo, w_bf16)


def _ln_kernel(x_ref, g_ref, b_ref, o_ref):
    o_ref[...] = _ln_rows(x_ref[...], g_ref[...], b_ref[...])


def _embed_ln(x, g, b, tm):
    n = x.shape[0]
    return pl.pallas_call(
        _ln_kernel,
        grid=(n // tm,),
        in_specs=[pl.BlockSpec((tm, D_MODEL), lambda i: (i, 0)),
                  pl.BlockSpec((1, D_MODEL), lambda i: (0, 0)),
                  pl.BlockSpec((1, D_MODEL), lambda i: (0, 0))],
        out_specs=pl.BlockSpec((tm, D_MODEL), lambda i: (i, 0)),
        out_shape=jax.ShapeDtypeStruct((n, D_MODEL), F32),
        compiler_params=_cparams(("parallel",)),
        name="embed_ln",
    )(x, g.reshape(1, -1), b.reshape(1, -1))


def _proj_kernel(pos_ref, x_ref, ws_ref, wuq_ref, wukv_ref, gq_ref, gkv_ref, gaq_ref, gak_ref,
                 bd_ref, tab_ref,
                 qm_ref, km_ref, vm_ref, wq_ref, wk_ref, wv_ref, aq_ref, ak_ref, av_ref):
    del pos_ref
    tm = x_ref.shape[0]
    x = x_ref[...].astype(BF16)
    res = _dot(x, ws_ref[...])

    m_cos = tab_ref[:, 0 * LANES:1 * LANES]
    m_sa = tab_ref[:, 1 * LANES:2 * LANES]
    m_sb = tab_ref[:, 2 * LANES:3 * LANES]
    a_cos = tab_ref[:, 3 * LANES:4 * LANES]
    a_sa = tab_ref[:, 4 * LANES:5 * LANES]
    a_sb = tab_ref[:, 5 * LANES:6 * LANES]
    c_q = _rms_rows(res[:, C_CQ:C_CKV], gq_ref[...]).astype(BF16)
    q = _dot(c_q, wuq_ref[...])
    q_scale = MLA_QK ** -0.5 * LOG2E
    for h in range(MLA_HEADS):
        base = h * MLA_QK_PAD
        qm_ref[:, base:base + LANES] = (q[:, base:base + LANES] * q_scale).astype(BF16)
        roped = _rope_chunk(q[:, base + LANES:base + 2 * LANES], m_cos, m_sa, m_sb, MLA_ROPE // 2)
        qm_ref[:, base + LANES:base + 2 * LANES] = (roped * q_scale).astype(BF16)

    c_kv = _rms_rows(res[:, C_CKV:C_KR], gkv_ref[...]).astype(BF16)
    kv = _dot(c_kv, wukv_ref[...])
    k_rope = _rope_chunk(res[:, C_KR:C_WQ], m_cos, m_sa, m_sb, MLA_ROPE // 2).astype(BF16)
    for h in range(MLA_HEADS):
        base = h * MLA_QK_PAD
        km_ref[:, base:base + LANES] = kv[:, h * LANES:(h + 1) * LANES].astype(BF16)
        km_ref[:, base + LANES:base + 2 * LANES] = k_rope
    v_off = MLA_HEADS * MLA_NOPE
    for r in range(tm // ATT_TK):
        rows = slice(r * ATT_TK, (r + 1) * ATT_TK)
        for h in range(MLA_HEADS):
            vm_ref[r, h * MLA_V:(h + 1) * MLA_V, :] = (
                kv[rows, v_off + h * MLA_V:v_off + (h + 1) * MLA_V].T.astype(BF16))
        av_ref[r] = res[rows, C_AV:C_END].T.astype(BF16)

    wq_ref[...] = (res[:, C_WQ:C_WK] * (HEAD_DIM ** -0.5 * LOG2E)).astype(BF16)
    wk_ref[...] = res[:, C_WK:C_WV].astype(BF16)
    for r in range(tm // Q_BLOCK):
        wv_ref[r] = res[r * Q_BLOCK:(r + 1) * Q_BLOCK, C_WV:C_AQ].T.astype(BF16)

    bd = bd_ref[...]

    def norm_rope(xc, g):
        ms = _split_dot(xc * xc, bd)
        xn = xc * lax.rsqrt(ms + RMS_EPS) * g
        return _rope_chunk(xn, a_cos, a_sa, a_sb, HEAD_DIM // 4)

    for c in range(QW // LANES):
        xc = res[:, C_AQ + c * LANES:C_AQ + (c + 1) * LANES]
        aq_ref[:, c * LANES:(c + 1) * LANES] = (
            norm_rope(xc, gaq_ref[...]) * (HEAD_DIM ** -0.5 * LOG2E)).astype(BF16)
    ak_ref[...] = norm_rope(res[:, C_AK:C_AV], gak_ref[...]).astype(BF16)


def _in_proj(x, pos_blk, lw, tab, tm):
    n = x.shape[0]
    const = lambda shape: pl.BlockSpec(shape, lambda i, p: (0, 0))
    rows = lambda w: pl.BlockSpec((tm, w), lambda i, p: (i, 0))
    out_widths = [MLA_HEADS * MLA_QK_PAD, MLA_HEADS * MLA_QK_PAD, (MLA_HEADS * MLA_V, ATT_TK),
                  QW, KW, (KW, Q_BLOCK), QW, KW, (KW, ATT_TK)]

    def out_spec(w):
        if isinstance(w, tuple):
            return pl.BlockSpec((tm // w[1], w[0], w[1]), lambda i, p: (i, 0, 0))
        return rows(w)

    def out_shape(w):
        if isinstance(w, tuple):
            return jax.ShapeDtypeStruct((n // w[1], w[0], w[1]), BF16)
        return jax.ShapeDtypeStruct((n, w), BF16)

    grid_spec = pltpu.PrefetchScalarGridSpec(
        num_scalar_prefetch=1,
        grid=(n // tm,),
        in_specs=[rows(D_MODEL),
                  const((D_MODEL, C_END)),
                  const((MLA_Q_RANK, MLA_HEADS * MLA_QK_PAD)),
                  const((MLA_KV_RANK, 2 * MLA_HEADS * LANES)),
                  const((1, MLA_Q_RANK)), const((1, MLA_KV_RANK)),
                  const((1, LANES)), const((1, LANES)),
                  const((LANES, LANES)),
                  pl.BlockSpec((tm, TAB_W), lambda i, p: (p[i], 0))],
        out_specs=[out_spec(w) for w in out_widths],
    )
    return pl.pallas_call(
        _proj_kernel,
        grid_spec=grid_spec,
        out_shape=[out_shape(w) for w in out_widths],
        compiler_params=_cparams(("parallel",)),
        name="in_proj",
    )(pos_blk, x, lw["w_s"], lw["w_uq"], lw["w_ukv"], lw["g_q"], lw["g_kv"], lw["g_aq"], lw["g_ak"],
      lw["bd"], tab)


def _call_into(prev, kernel, operands, *, in_specs, **kwargs):
    if prev is None:
        prev = jnp.zeros(kwargs["out_shape"].shape, kwargs["out_shape"].dtype)
    n_in = len(operands)

    def skip_prev(*refs):
        return kernel(*refs[:n_in], *refs[n_in + 1:])

    return pl.pallas_call(skip_prev, in_specs=list(in_specs) + [pl.BlockSpec(memory_space=pl.ANY)],
                          input_output_aliases={n_in: 0}, **kwargs)(*operands, prev)


def _flash_t(qt_sc, k_ref, vt_ref, s0_sc, s1_sc, acc_sc, seq, v_rows):
    n_blk = acc_sc.shape[0]
    nk = seq // ATT_TK
    assert nk >= 2 and nk % 2 == 0
    acc_sc[...] = jnp.zeros(acc_sc.shape, F32)
    stat = lambda v: tuple(jnp.full((1, MXU_DIM), v, F32) for _ in range(n_blk))

    def scores(j, s_ref):
        k = k_ref[pl.ds(pl.multiple_of(j * ATT_TK, ATT_TK), ATT_TK), :]
        tops = []
        for b in range(n_blk):
            s = _dot(k, qt_sc[:, b * MXU_DIM:(b + 1) * MXU_DIM])
            s_ref[b] = s
            tops.append(jnp.max(s, axis=0, keepdims=True))
        return tuple(tops)

    ones_rows = jnp.where(lax.broadcasted_iota(jnp.int32, (BF16_ROWS, ATT_TK), 0) == 0, 1.0, 0.0).astype(BF16)

    def consume(j, s_ref, tops, ms):
        ms = list(ms)
        for b in range(n_blk):
            m_new = jnp.maximum(ms[b], tops[b])
            alpha = jnp.exp2(ms[b] - m_new)
            p = jnp.exp2((s_ref[b] - m_new).astype(BF16))
            ms[b] = m_new
            r0, r1 = v_rows[b]
            vt = jnp.concatenate([vt_ref[j, r0:r1, :], ones_rows], axis=0)
            acc_sc[b] = acc_sc[b] * alpha + _dot(vt, p)
        return tuple(ms)

    def pair(i, carry):
        ms, tops0 = carry
        tops1 = scores(2 * i + 1, s1_sc)
        ms = consume(2 * i, s0_sc, tops0, ms)
        tops0 = scores(2 * i + 2, s0_sc)
        return consume(2 * i + 1, s1_sc, tops1, ms), tops0

    trips = nk // 2 - 1
    unroll = trips if trips <= 5 else next(u for u in (5, 4, 3, 2, 1) if trips % u == 0)
    ms, tops0 = lax.fori_loop(0, trips, pair, (stat(NEG_INF), scores(0, s0_sc)), unroll=max(unroll, 1))
    tops1 = scores(nk - 1, s1_sc)
    ms = consume(nk - 2, s0_sc, tops0, ms)
    consume(nk - 1, s1_sc, tops1, ms)


def _stage_queries(q_ref, qt_sc):
    tq = q_ref.shape[0]
    lo = lax.broadcasted_iota(jnp.int32, (LANES, tq), 0) < HEAD_DIM
    for c in range(GROUP):
        qt = q_ref[:, c * LANES:(c + 1) * LANES].astype(F32).T
        qt_sc[:, c * tq:(c + 1) * tq] = jnp.where(lo, qt, 0.0).astype(BF16)
        qt_sc[:, (GROUP + c) * tq:(GROUP + c + 1) * tq] = jnp.where(lo, 0.0, qt).astype(BF16)


def _store_heads(head_out, o_ref):
    for c in range(GROUP):
        ot = jnp.concatenate([head_out(c), head_out(GROUP + c)], axis=0)
        o_ref[:, c * LANES:(c + 1) * LANES] = ot.T.astype(o_ref.dtype)


AXIAL_TQ = 128
AXIAL_SWEEPS = 4


def _gqa_kernel(q_ref, k_ref, vt_ref, o_ref, *scratch, seq):
    tq = AXIAL_TQ
    heads_per_blk = MXU_DIM // tq
    for t in range(AXIAL_SWEEPS):
        qt_sc, s0_sc, s1_sc, acc_sc = scratch[4 * t:4 * t + 4]
        rows = pl.ds(t * tq, tq)
        _stage_queries(q_ref.at[rows], qt_sc)
        v_rows = [((b * heads_per_blk) // GROUP * HEAD_DIM, ((b * heads_per_blk) // GROUP + 1) * HEAD_DIM)
                  for b in range(acc_sc.shape[0])]
        _flash_t(qt_sc, k_ref, vt_ref, s0_sc, s1_sc, acc_sc, seq, v_rows)

        def head_out(h, acc_sc=acc_sc):
            b, off = divmod(h * tq, MXU_DIM)
            return acc_sc[b, 0:HEAD_DIM, off:off + tq] / acc_sc[b, HEAD_DIM:HEAD_DIM + 1, off:off + tq]

        _store_heads(head_out, o_ref.at[rows])


def _axial_attention(q, k, vt, prev, tok_off, batch, seq):
    tq = AXIAL_TQ * AXIAL_SWEEPS
    nq = seq // tq
    nkb = seq // ATT_TK
    qb, sb = tok_off // tq, tok_off // seq
    n_blk = HEADS * AXIAL_TQ // MXU_DIM
    sweep_scratch = [pltpu.VMEM((KW, HEADS * AXIAL_TQ), BF16),
                     pltpu.VMEM((n_blk, ATT_TK, MXU_DIM), F32),
                     pltpu.VMEM((n_blk, ATT_TK, MXU_DIM), F32),
                     pltpu.VMEM((n_blk, HEAD_DIM + BF16_ROWS, MXU_DIM), F32)]
    return _call_into(
        prev, functools.partial(_gqa_kernel, seq=seq), (q, k, vt),
        grid=(batch, nq),
        in_specs=[pl.BlockSpec((tq, QW), lambda b, i: (qb + b * nq + i, 0)),
                  pl.BlockSpec((seq, KW), lambda b, i: (sb + b, 0)),
                  pl.BlockSpec((nkb, KW, ATT_TK), lambda b, i: (sb + b, 0, 0))],
        out_specs=pl.BlockSpec((tq, QW), lambda b, i: (qb + b * nq + i, 0)),
        out_shape=jax.ShapeDtypeStruct((q.shape[0], QW), BF16),
        scratch_shapes=sweep_scratch * AXIAL_SWEEPS,
        compiler_params=_cparams(("parallel", "arbitrary"), flags=ATT_FLAGS),
        name="axial_attention",
    )


MLA_SWEEP_BLOCKS = 4
MLA_SWEEPS = 2


def _mla_kernel(q_ref, k_ref, vt_ref, o_ref, *scratch, seq):
    sweeps = len(scratch) // 4
    n_blk = q_ref.shape[0] // (sweeps * MXU_DIM)
    for t in range(sweeps):
        qt_sc, s0_sc, s1_sc, acc_sc = scratch[4 * t:4 * t + 4]
        for b in range(n_blk):
            rows = pl.ds((t * n_blk + b) * MXU_DIM, MXU_DIM)
            qt_sc[:, b * MXU_DIM:(b + 1) * MXU_DIM] = q_ref[rows, :].astype(F32).T.astype(BF16)
        _flash_t(qt_sc, k_ref, vt_ref, s0_sc, s1_sc, acc_sc, seq, [(0, MLA_V)] * n_blk)
        for b in range(n_blk):
            rows = pl.ds((t * n_blk + b) * MXU_DIM, MXU_DIM)
            o = acc_sc[b, 0:MLA_V, :] / acc_sc[b, MLA_V:MLA_V + 1, :]
            o_ref[rows, :] = o.T.astype(o_ref.dtype)


def _mla_attention(q, k, vt, prev, tok_off, batch, seq):
    tq = _pick_tile(seq, MLA_SWEEPS * MLA_SWEEP_BLOCKS * MXU_DIM)
    sweeps = MLA_SWEEPS if tq % (MLA_SWEEPS * MXU_DIM) == 0 else 1
    n_blk = tq // (sweeps * MXU_DIM)
    assert n_blk * sweeps * MXU_DIM == tq
    nq = seq // tq
    nkb = seq // ATT_TK
    qb, sb = tok_off // tq, tok_off // seq
    sweep_scratch = [pltpu.VMEM((MLA_QK_PAD, n_blk * MXU_DIM), BF16),
                     pltpu.VMEM((n_blk, ATT_TK, MXU_DIM), F32),
                     pltpu.VMEM((n_blk, ATT_TK, MXU_DIM), F32),
                     pltpu.VMEM((n_blk, MLA_V + BF16_ROWS, MXU_DIM), F32)]
    return _call_into(
        prev, functools.partial(_mla_kernel, seq=seq), (q, k, vt),
        grid=(batch, MLA_HEADS, nq),
        in_specs=[pl.BlockSpec((tq, MLA_QK_PAD), lambda b, h, i: (qb + b * nq + i, h)),
                  pl.BlockSpec((seq, MLA_QK_PAD), lambda b, h, i: (sb + b, h)),
                  pl.BlockSpec((nkb, MLA_V, ATT_TK), lambda b, h, i: (sb + b, h, 0))],
        out_specs=pl.BlockSpec((tq, MLA_V), lambda b, h, i: (qb + b * nq + i, h)),
        out_shape=jax.ShapeDtypeStruct((q.shape[0], MLA_HEADS * MLA_V), BF16),
        scratch_shapes=sweep_scratch * sweeps,
        compiler_params=_cparams(("parallel", "parallel", "arbitrary"), flags=ATT_FLAGS),
        name="mla_attention",
    )


WINDOW_SWEEPS = 4


def _window_kernel(sink_ref, q_ref, k_ref, vt_ref, o_ref, *qt_scs, seq):
    for t, qt_sc in enumerate(qt_scs):
        rows = pl.ds(t * Q_BLOCK, Q_BLOCK)
        _window_block(sink_ref, q_ref.at[rows], k_ref, vt_ref, o_ref.at[rows], qt_sc,
                      (pl.program_id(1) * len(qt_scs) + t) * Q_BLOCK, seq)


def _window_block(sink_ref, q_ref, k_ref, vt_ref, o_ref, qt_sc, start, seq):
    kstart = pl.multiple_of(jnp.clip(start - WINDOW, 0, seq - SPAN), Q_BLOCK)
    kb = kstart // Q_BLOCK
    _stage_queries(q_ref, qt_sc)
    s_all = _dot(k_ref[pl.ds(kstart, SPAN), :], qt_sc[...])
    vt = jnp.concatenate([vt_ref[kb + j] for j in range(SPAN // Q_BLOCK)], axis=1)
    ones_rows = jnp.where(lax.broadcasted_iota(jnp.int32, (BF16_ROWS, SPAN), 0) == 0, 1.0, 0.0).astype(BF16)
    rel = ((kstart + lax.broadcasted_iota(jnp.int32, (SPAN, Q_BLOCK), 0))
           - (start + lax.broadcasted_iota(jnp.int32, (SPAN, Q_BLOCK), 1)))
    dist_i = jnp.abs(rel)
    dist = jnp.where(dist_i <= WINDOW, dist_i.astype(F32), -NEG_INF * 2.0 ** HEADS)
    outs = []
    for blk in range(HEADS // 2):
        ps, sinks = [], []
        for h in (2 * blk, 2 * blk + 1):
            slope = 2.0 ** (-8.0 * (h + 1) / HEADS) * LOG2E
            sink = sink_ref[h] * LOG2E
            logits = s_all[:, h * Q_BLOCK:(h + 1) * Q_BLOCK] - slope * dist
            m = jnp.maximum(jnp.max(logits, axis=0, keepdims=True), sink)
            ps.append(jnp.exp2((logits - m).astype(BF16)))
            sinks.append(jnp.exp2(sink - m))
        g = (2 * blk) // GROUP
        lhs = jnp.concatenate([vt[g * HEAD_DIM:(g + 1) * HEAD_DIM], ones_rows], axis=0)
        pv = _dot(lhs, jnp.concatenate(ps, axis=1))
        denom = pv[HEAD_DIM:HEAD_DIM + 1] + jnp.concatenate(sinks, axis=1)
        outs.append(pv[0:HEAD_DIM] / denom)
    _store_heads(lambda h: outs[h // 2][:, (h % 2) * Q_BLOCK:(h % 2 + 1) * Q_BLOCK], o_ref)


def _window_attention(q, k, v, sink, prev, tok_off, batch, seq):
    tq = WINDOW_SWEEPS * Q_BLOCK
    assert seq >= SPAN and seq % tq == 0
    nq = seq // tq
    qb, sb = tok_off // tq, tok_off // seq
    return _call_into(
        prev, functools.partial(_window_kernel, seq=seq), (sink, q, k, v),
        grid=(batch, nq),
        in_specs=[pl.BlockSpec(memory_space=pltpu.SMEM),
                  pl.BlockSpec((tq, QW), lambda b, i: (qb + b * nq + i, 0)),
                  pl.BlockSpec((seq, KW), lambda b, i: (sb + b, 0)),
                  pl.BlockSpec((seq // Q_BLOCK, KW, Q_BLOCK), lambda b, i: (sb + b, 0, 0))],
        out_specs=pl.BlockSpec((tq, QW), lambda b, i: (qb + b * nq + i, 0)),
        out_shape=jax.ShapeDtypeStruct((q.shape[0], QW), BF16),
        scratch_shapes=[pltpu.VMEM((KW, HEADS * Q_BLOCK), BF16)] * WINDOW_SWEEPS,
        compiler_params=_cparams(("parallel", "arbitrary")),
        name="window_attention",
    )


def _route(x, w_hi, w_lo):
    hi = x.astype(BF16)
    lo = (x - hi.astype(F32)).astype(BF16)
    logits = _dot(hi, w_hi) + _dot(lo, w_hi) + _dot(hi, w_lo)
    lane = lax.broadcasted_iota(jnp.int32, logits.shape, 1)
    logits = jnp.where(lane < N_EXPERTS, logits, NEG_INF)
    t1 = jnp.max(logits, axis=1, keepdims=True)
    e1 = jnp.min(jnp.where(logits == t1, lane, LANES), axis=1, keepdims=True)
    rest = jnp.where(lane == e1, NEG_INF, logits)
    t2 = jnp.max(rest, axis=1, keepdims=True)
    e2 = jnp.min(jnp.where(rest == t2, lane, LANES), axis=1, keepdims=True)
    w = jnp.exp(t2 - t1)
    g1 = 1.0 / (1.0 + w)
    g2 = w / (1.0 + w)
    return (jnp.where(lane == 0, e1, jnp.where(lane == 1, e2, 0)),
            jnp.where(lane == 0, g1, jnp.where(lane == 1, g2, 0.0)))


def _merge_kernel(x_ref, oa_ref, ob_ref, oc_ref, wg_ref, wa_ref, wb_ref, wc_ref, wo_ref, g_ref, b_ref,
                  *rest, alpha, route):
    x = x_ref[...]
    xb = x.astype(BF16)
    merged = None
    for idx, (o_br, w_br) in enumerate(((oa_ref, wa_ref), (ob_ref, wb_ref), (oc_ref, wc_ref))):
        gate = jax.nn.sigmoid(_dot(xb, wg_ref[:, idx * D_MODEL:(idx + 1) * D_MODEL]))
        term = gate * _dot(o_br[...], w_br[...])
        merged = term if merged is None else merged + term
    m = _dot(merged.astype(BF16), wo_ref[...])
    y = _ln_rows(alpha * x + m, g_ref[...], b_ref[...])
    if route:
        whi_ref, wlo_ref, o_ref, e_ref, gate_ref = rest
        e_ref[...], gate_ref[...] = _route(y, whi_ref[...], wlo_ref[...])
    else:
        (o_ref,) = rest
    o_ref[...] = y


def _merge(x, o_a, o_b, o_c, lw, alpha, tm, router=None):
    n = x.shape[0]
    const = lambda shape: pl.BlockSpec(shape, lambda i: (0, 0))
    rows = lambda w: pl.BlockSpec((tm, w), lambda i: (i, 0))
    in_specs = [rows(D_MODEL), rows(MLA_HEADS * MLA_V), rows(QW), rows(QW),
                const((D_MODEL, N_BRANCH * D_MODEL)),
                const((MLA_HEADS * MLA_V, D_MODEL)), const((QW, D_MODEL)), const((QW, D_MODEL)),
                const((D_MODEL, D_MODEL)), const((1, D_MODEL)), const((1, D_MODEL))]
    operands = [x, o_a, o_b, o_c, lw["w_g"], lw["w_br_mla"], lw["w_br_win"], lw["w_br_ax"], lw["w_out"],
                lw["ln1_g"], lw["ln1_b"]]
    out_specs, out_shape = rows(D_MODEL), jax.ShapeDtypeStruct((n, D_MODEL), F32)
    if router is not None:
        in_specs += [const((D_MODEL, LANES)), const((D_MODEL, LANES))]
        operands += list(router)
        out_specs = [out_specs, rows(LANES), rows(LANES)]
        out_shape = [out_shape, jax.ShapeDtypeStruct((n, LANES), jnp.int32),
                     jax.ShapeDtypeStruct((n, LANES), F32)]
    return pl.pallas_call(
        functools.partial(_merge_kernel, alpha=alpha, route=router is not None),
        grid=(n // tm,),
        in_specs=in_specs,
        out_specs=out_specs,
        out_shape=out_shape,
        compiler_params=_cparams(("parallel",)),
        name="gated_merge",
    )(*operands)


def _ffn_kernel(x_ref, w1_ref, w3_ref, w2_ref, g_ref, b_ref, o_ref, acc_ref, *, alpha):
    f = pl.program_id(1)

    @pl.when(f == 0)
    def _():
        acc_ref[...] = jnp.zeros(acc_ref.shape, F32)

    xb = x_ref[...].astype(BF16)
    h = jax.nn.silu(_dot(xb, w1_ref[...])) * _dot(xb, w3_ref[...])
    acc_ref[...] += _dot(h.astype(BF16), w2_ref[...])

    @pl.when(f == pl.num_programs(1) - 1)
    def _():
        o_ref[...] = _ln_rows(alpha * x_ref[...] + acc_ref[...], g_ref[...], b_ref[...])


def _dense_ffn(x, w1, w3, w2, g, b, alpha, tm, tf):
    n = x.shape[0]
    d_ff = w1.shape[1]
    mode = dict(pipeline_mode=pl.Buffered(1)) if tf == d_ff else {}
    return pl.pallas_call(
        functools.partial(_ffn_kernel, alpha=alpha),
        grid=(n // tm, d_ff // tf),
        in_specs=[pl.BlockSpec((tm, D_MODEL), lambda i, f: (i, 0)),
                  pl.BlockSpec((D_MODEL, tf), lambda i, f: (0, f), **mode),
                  pl.BlockSpec((D_MODEL, tf), lambda i, f: (0, f), **mode),
                  pl.BlockSpec((tf, D_MODEL), lambda i, f: (f, 0), **mode),
                  pl.BlockSpec((1, D_MODEL), lambda i, f: (0, 0)),
                  pl.BlockSpec((1, D_MODEL), lambda i, f: (0, 0))],
        out_specs=pl.BlockSpec((tm, D_MODEL), lambda i, f: (i, 0)),
        out_shape=jax.ShapeDtypeStruct((n, D_MODEL), F32),
        scratch_shapes=[pltpu.VMEM((tm, D_MODEL), F32)],
        compiler_params=_cparams(("parallel", "arbitrary")),
        name="dense_ffn",
    )(x, w1, w3, w2, g, b)


ROW_TILE = 8


def _row_of(ref3, r):
    return ref3.at[lax.shift_right_logical(r, 3), pl.ds(jnp.bitwise_and(r, ROW_TILE - 1), 1)]


def _dispatch_kernel(pad_ref, dest_hbm, x_ref, xs_hbm, dest_smem, zero_ref, idx_sem, row_sem, *, n_pad, zpad):
    i = pl.program_id(0)
    groups = x_ref.shape[0]
    n_idx = TOP_K * ROW_TILE * groups
    cp = pltpu.make_async_copy(dest_hbm.at[pl.ds(pl.multiple_of(i * n_idx, n_idx), n_idx)], dest_smem, idx_sem)
    cp.start()
    zero_ref[...] = jnp.zeros(zero_ref.shape, F32)
    cp.wait()

    def send(g, carry):
        for u in range(ROW_TILE):
            for k in range(TOP_K):
                d = dest_smem[(g * ROW_TILE + u) * TOP_K + k]
                pltpu.make_async_copy(x_ref.at[g, pl.ds(u, 1)], _row_of(xs_hbm, d), row_sem).start(priority=k)
        return carry

    lax.fori_loop(0, groups, send, 0)

    def pad_copy(r):
        return pltpu.make_async_copy(zero_ref.at[pl.ds(0, 1)], _row_of(xs_hbm, pad_ref[i * zpad + r]), row_sem)

    def send_zero(r, carry):
        @pl.when(i * zpad + r < n_pad)
        def _():
            pad_copy(r).start()
        return carry

    def drain_zero(r, carry):
        @pl.when(i * zpad + r < n_pad)
        def _():
            pad_copy(r).wait()
        return carry

    def drain(r, carry):
        pltpu.make_async_copy(x_ref.at[0, pl.ds(0, 1)], xs_hbm.at[0, pl.ds(0, 1)], row_sem).wait()
        return carry

    lax.fori_loop(0, zpad, send_zero, 0)
    lax.fori_loop(0, n_idx, drain, 0, unroll=8)
    lax.fori_loop(0, zpad, drain_zero, 0)


def _dispatch_rows(x, dest, pad_slots, p, tm):
    n = x.shape[0]
    steps = n // tm
    n_pad = pad_slots.shape[0]
    zpad = -(-n_pad // steps)
    grid_spec = pltpu.PrefetchScalarGridSpec(
        num_scalar_prefetch=1,
        grid=(steps,),
        in_specs=[pl.BlockSpec(memory_space=pl.ANY),
                  pl.BlockSpec((tm // ROW_TILE, ROW_TILE, D_MODEL), lambda i, pad: (i, 0, 0))],
        out_specs=pl.BlockSpec(memory_space=pl.ANY),
        scratch_shapes=[pltpu.SMEM((TOP_K * tm,), jnp.int32),
                        pltpu.VMEM((ROW_TILE, D_MODEL), F32),
                        pltpu.SemaphoreType.DMA, pltpu.SemaphoreType.DMA],
    )
    xs = pl.pallas_call(
        functools.partial(_dispatch_kernel, n_pad=n_pad, zpad=zpad),
        grid_spec=grid_spec,
        out_shape=jax.ShapeDtypeStruct((p // ROW_TILE, ROW_TILE, D_MODEL), F32),
        compiler_params=_cparams(("arbitrary",)),
        name="moe_dispatch",
    )(pad_slots, dest, x.reshape(n // ROW_TILE, ROW_TILE, D_MODEL))
    return xs.reshape(p, D_MODEL)


def _expert_kernel(be_ref, bv_ref, x_ref, w1_ref, w3_ref, w2_ref, o_ref, acc_ref):
    del be_ref
    i = pl.program_id(0)
    f = pl.program_id(1)
    valid = bv_ref[i] != 0

    @pl.when(f == 0)
    def _():
        acc_ref[...] = jnp.zeros(acc_ref.shape, F32)

    @pl.when(valid)
    def _():
        xb = x_ref[...].astype(BF16)
        h = jax.nn.silu(_dot(xb, w1_ref[0])) * _dot(xb, w3_ref[0])
        acc_ref[...] += _dot(h.astype(BF16), w2_ref[0])

    @pl.when(f == pl.num_programs(1) - 1)
    def _():
        o_ref[...] = acc_ref[...]


def _expert_ffn(xs, blk_expert, blk_valid, w1, w3, w2, tm, tf):
    p = xs.shape[0]
    d_ff = w1.shape[2]
    nf = d_ff // tf
    fidx = lambda i, f, be, bv: jnp.where(bv[i] != 0, f, nf - 1)
    grid_spec = pltpu.PrefetchScalarGridSpec(
        num_scalar_prefetch=2,
        grid=(p // tm, nf),
        in_specs=[pl.BlockSpec((tm, D_MODEL), lambda i, f, be, bv: (i, 0)),
                  pl.BlockSpec((1, D_MODEL, tf), lambda i, f, be, bv: (be[i], 0, fidx(i, f, be, bv))),
                  pl.BlockSpec((1, D_MODEL, tf), lambda i, f, be, bv: (be[i], 0, fidx(i, f, be, bv))),
                  pl.BlockSpec((1, tf, D_MODEL), lambda i, f, be, bv: (be[i], fidx(i, f, be, bv), 0))],
        out_specs=pl.BlockSpec((tm, D_MODEL), lambda i, f, be, bv: (i, 0)),
        scratch_shapes=[pltpu.VMEM((tm, D_MODEL), F32)],
    )
    return pl.pallas_call(
        _expert_kernel,
        grid_spec=grid_spec,
        out_shape=jax.ShapeDtypeStruct((p, D_MODEL), F32),
        compiler_params=_cparams(("arbitrary", "arbitrary")),
        name="moe_experts",
    )(blk_expert, blk_valid, xs, w1, w3, w2)


def _combine_kernel(pos_hbm, ys_hbm, x_ref, gate_ref, g_ref, b_ref, o_ref,
                    r0_ref, r1_ref, pos_smem, pos_sem, row_sem, *, alpha):
    i = pl.program_id(0)
    tm = x_ref.shape[0]
    n_idx = TOP_K * tm
    cp = pltpu.make_async_copy(pos_hbm.at[pl.ds(pl.multiple_of(i * n_idx, n_idx), n_idx)], pos_smem, pos_sem)
    cp.start()
    cp.wait()

    def fetch(g, carry):
        for u in range(ROW_TILE):
            for k, buf in enumerate((r0_ref, r1_ref)):
                src = _row_of(ys_hbm, pos_smem[(g * ROW_TILE + u) * TOP_K + k])
                pltpu.make_async_copy(src, buf.at[g, pl.ds(u, 1)], row_sem).start(priority=k)
        return carry

    def drain(r, carry):
        pltpu.make_async_copy(ys_hbm.at[0, pl.ds(0, 1)], r0_ref.at[0, pl.ds(0, 1)], row_sem).wait()
        return carry

    lax.fori_loop(0, tm // ROW_TILE, fetch, 0)
    lax.fori_loop(0, n_idx, drain, 0, unroll=8)
    gates = gate_ref[...]
    y = (gates[:, 0:1] * r0_ref[...].reshape(tm, D_MODEL) + gates[:, 1:2] * r1_ref[...].reshape(tm, D_MODEL))
    o_ref[...] = _ln_rows(alpha * x_ref[...] + y, g_ref[...], b_ref[...])


def _moe_combine(x, ys, pos, gates, g, b, alpha, tm):
    n = x.shape[0]
    return pl.pallas_call(
        functools.partial(_combine_kernel, alpha=alpha),
        grid=(n // tm,),
        in_specs=[pl.BlockSpec(memory_space=pl.ANY), pl.BlockSpec(memory_space=pl.ANY),
                  pl.BlockSpec((tm, D_MODEL), lambda i: (i, 0)),
                  pl.BlockSpec((tm, LANES), lambda i: (i, 0)),
                  pl.BlockSpec((1, D_MODEL), lambda i: (0, 0)),
                  pl.BlockSpec((1, D_MODEL), lambda i: (0, 0))],
        out_specs=pl.BlockSpec((tm, D_MODEL), lambda i: (i, 0)),
        out_shape=jax.ShapeDtypeStruct((n, D_MODEL), F32),
        scratch_shapes=[pltpu.VMEM((tm // ROW_TILE, ROW_TILE, D_MODEL), F32),
                        pltpu.VMEM((tm // ROW_TILE, ROW_TILE, D_MODEL), F32),
                        pltpu.SMEM((TOP_K * tm,), jnp.int32),
                        pltpu.SemaphoreType.DMA, pltpu.SemaphoreType.DMA],
        compiler_params=_cparams(("arbitrary",)),
        name="moe_combine",
    )(pos, ys.reshape(ys.shape[0] // ROW_TILE, ROW_TILE, D_MODEL), x, gates, g, b)


def _moe_layer(x, e_out, gates, w1, w3, w2, g, b, alpha, tm, tm_e, tf):
    n = x.shape[0]
    a = n * TOP_K
    e_flat = e_out[:, :TOP_K].reshape(a)
    onehot = (e_flat[:, None] == jnp.arange(N_EXPERTS, dtype=jnp.int32)[None, :]).astype(jnp.int32)
    csum = jnp.cumsum(onehot, axis=0)
    counts = csum[-1]
    rank = jnp.sum((csum - 1) * onehot, axis=1)
    padded = (counts + tm_e - 1) // tm_e * tm_e
    pad_end = jnp.cumsum(padded)
    pad_start = pad_end - padded
    dest = (pad_start[e_flat] + rank).astype(jnp.int32)
    assert a % tm_e == 0
    n_pad = N_EXPERTS * tm_e
    p = a + n_pad
    npad = padded - counts
    pad_cum = jnp.cumsum(npad)
    pad_i = jnp.arange(n_pad, dtype=jnp.int32)
    pad_e = jnp.searchsorted(pad_cum, pad_i, side="right")
    pad_ec = jnp.minimum(pad_e, N_EXPERTS - 1)
    pad_slots = jnp.where(pad_e >= N_EXPERTS, pad_end[-1] + (pad_i - pad_cum[-1]),
                          (pad_start + counts)[pad_ec] + (pad_i - (pad_cum - npad)[pad_ec])).astype(jnp.int32)
    blk_start = jnp.arange(p // tm_e, dtype=jnp.int32) * tm_e
    blk_valid = (blk_start < pad_end[-1]).astype(jnp.int32)
    blk_expert = jnp.minimum(jnp.searchsorted(pad_end, jnp.minimum(blk_start, pad_end[-1] - 1), side="right"),
                             N_EXPERTS - 1).astype(jnp.int32)

    xs = _dispatch_rows(x, dest, pad_slots, p, tm)
    ys = _expert_ffn(xs, blk_expert, blk_valid, w1, w3, w2, tm_e, tf)
    return _moe_combine(x, ys, dest, gates, g, b, alpha, tm)


def _rope_tables(t_max):
    t = jnp.arange(t_max, dtype=jnp.int32)

    def freqs(pos, dim):
        inv = ROPE_THETA ** (-jnp.arange(0, dim, 2, dtype=F32) / dim)
        ang = pos.astype(F32)[:, None] * inv[None, :]
        return jnp.cos(ang), jnp.sin(ang)

    zeros = lambda w: jnp.zeros((t_max, w), F32)
    c, s = freqs(t, MLA_ROPE)
    half = MLA_ROPE // 2
    m_cos = jnp.concatenate([c, c, zeros(LANES - MLA_ROPE)], axis=1)
    m_sa = jnp.concatenate([-s, zeros(LANES - half)], axis=1)
    m_sb = jnp.concatenate([zeros(half), s, zeros(LANES - MLA_ROPE)], axis=1)
    cr, sr = freqs(t // GRID_W, HEAD_DIM // 2)
    cc, sc = freqs(t % GRID_W, HEAD_DIM // 2)
    q = HEAD_DIM // 4
    head_cos = jnp.concatenate([cr, cr, cc, cc], axis=1)
    head_sa = jnp.concatenate([-sr, zeros(q), -sc, zeros(q)], axis=1)
    head_sb = jnp.concatenate([zeros(q), sr, zeros(q), sc], axis=1)
    rep = lambda x: jnp.concatenate([x] * (LANES // HEAD_DIM), axis=1)
    return jnp.concatenate([m_cos, m_sa, m_sb, rep(head_cos), rep(head_sa), rep(head_sb)], axis=1)


def _chunk_heads(w, axis):
    shape = w.shape
    w = w.reshape(shape[:axis] + (HEADS, HEAD_DIM) + shape[axis + 1:])
    w = jnp.take(w, jnp.array(HEAD_PERM), axis=axis)
    return w.reshape(shape)


def _prep_layer(l, w_in, mla_q_norm, mla_kv_norm, w_uq, w_ukv, ax_q_norm, ax_k_norm,
                w_br_mla, w_br_win, w_br_ax, w_out, ln1_g, ln1_b, ln2_g, ln2_b):
    wi = w_in[l]
    sizes = [MLA_Q_RANK, MLA_KV_RANK, MLA_ROPE, QW, KW, KW, QW, KW, KW, N_BRANCH * D_MODEL]
    offs = [0]
    for s in sizes:
        offs.append(offs[-1] + s)
    cols = [wi[:, offs[j]:offs[j + 1]] for j in range(len(sizes))]
    c_q, c_kv, k_r, wq, wk, wv, aq, ak, av, wg = cols
    w_s = jnp.concatenate([c_q, c_kv, k_r, jnp.zeros((D_MODEL, LANES - MLA_ROPE), F32),
                           _chunk_heads(wq, 1), wk, wv, _chunk_heads(aq, 1), ak, av], axis=1).astype(BF16)
    uq = w_uq[l].reshape(MLA_Q_RANK, MLA_HEADS, MLA_QK)
    uq = jnp.pad(uq, ((0, 0), (0, 0), (0, MLA_QK_PAD - MLA_QK))).reshape(MLA_Q_RANK, MLA_HEADS * MLA_QK_PAD)
    ukv = w_ukv[l].reshape(MLA_KV_RANK, MLA_HEADS, MLA_NOPE + MLA_V)
    ukv = jnp.concatenate([ukv[:, :, :MLA_NOPE].reshape(MLA_KV_RANK, -1),
                           ukv[:, :, MLA_NOPE:].reshape(MLA_KV_RANK, -1)], axis=1)
    head_id = jnp.arange(LANES) // HEAD_DIM
    bd = jnp.where(head_id[:, None] == head_id[None, :], 1.0 / HEAD_DIM, 0.0).astype(BF16)
    row = lambda v: v.reshape(1, -1).astype(F32)
    return dict(
        w_s=w_s, w_g=wg.astype(BF16), w_uq=uq.astype(BF16), w_ukv=ukv.astype(BF16),
        g_q=row(mla_q_norm[l]), g_kv=row(mla_kv_norm[l]),
        g_aq=row(jnp.tile(ax_q_norm[l], LANES // HEAD_DIM)), g_ak=row(jnp.tile(ax_k_norm[l], LANES // HEAD_DIM)),
        bd=bd,
        w_br_mla=w_br_mla[l].astype(BF16),
        w_br_win=_chunk_heads(w_br_win[l], 0).astype(BF16),
        w_br_ax=_chunk_heads(w_br_ax[l], 0).astype(BF16),
        w_out=w_out[l].astype(BF16),
        ln1_g=row(ln1_g[l]), ln1_b=row(ln1_b[l]), ln2_g=row(ln2_g[l]), ln2_b=row(ln2_b[l]),
    )


def kernel(x_prompt, x_sample, emb_ln_g, emb_ln_b, w_in, mla_q_norm, mla_kv_norm, w_uq, w_ukv, win_sink,
           ax_q_norm, ax_k_norm, w_br_mla, w_br_win, w_br_ax, w_out, ln1_g, ln1_b, ln2_g, ln2_b,
           ffn_w1, ffn_w3, ffn_w2, moe_router, moe_w1, moe_w3, moe_w2):
    depth = w_in.shape[0]
    alpha = (2 * depth) ** 0.25
    groups = []
    off = 0
    for xg in (x_prompt, x_sample):
        bsz, seq, _ = xg.shape
        assert off % seq == 0 and seq % GRID_W == 0
        groups.append((off, bsz, seq))
        off += bsz * seq
    n = off
    seq_min = min(g[2] for g in groups)
    seq_max = max(g[2] for g in groups)
    tm = _pick_tile(seq_min, 512)
    tm_e = 512
    x = jnp.concatenate([x_prompt.reshape(-1, D_MODEL), x_sample.reshape(-1, D_MODEL)], axis=0)

    tab = _rope_tables(seq_max)
    pos_blk = jnp.concatenate([jnp.tile(jnp.arange(seq // tm, dtype=jnp.int32), bsz)
                               for (_, bsz, seq) in groups])

    x = _embed_ln(x, emb_ln_g, emb_ln_b, tm)
    for l in range(depth):
        lw = _prep_layer(l, w_in, mla_q_norm, mla_kv_norm, w_uq, w_ukv, ax_q_norm, ax_k_norm,
                         w_br_mla, w_br_win, w_br_ax, w_out, ln1_g, ln1_b, ln2_g, ln2_b)
        qm, km, vm, wq, wk, wv, aq, ak, av = _in_proj(x, pos_blk, lw, tab, tm)
        sink = win_sink[l].astype(F32)
        o_a = o_b = o_c = None
        for g in groups:
            o_a = _mla_attention(qm, km, vm, o_a, *g)
            o_b = _window_attention(wq, wk, wv, sink, o_b, *g)
            o_c = _axial_attention(aq, ak, av, o_c, *g)
        i = l // 2
        if l % 2 == 0:
            x = _merge(x, o_a, o_b, o_c, lw, alpha, tm)
            d_ff = ffn_w1.shape[2]
            x = _dense_ffn(x, ffn_w1[i].astype(BF16), ffn_w3[i].astype(BF16), ffn_w2[i].astype(BF16),
                           lw["ln2_g"], lw["ln2_b"], alpha, tm, d_ff)
        else:
            router = jnp.pad(moe_router[i], ((0, 0), (0, LANES - N_EXPERTS)))
            r_hi = router.astype(BF16)
            r_lo = (router - r_hi.astype(F32)).astype(BF16)
            d_ffe = moe_w1.shape[3]
            x, e_out, gates = _merge(x, o_a, o_b, o_c, lw, alpha, tm, router=(r_hi, r_lo))
            x = _moe_layer(x, e_out, gates, moe_w1[i].astype(BF16), moe_w3[i].astype(BF16),
                           moe_w2[i].astype(BF16), lw["ln2_g"], lw["ln2_b"], alpha, tm, tm_e,
                           d_ffe // 2)
    n_p = x_prompt.shape[0] * x_prompt.shape[1]
    return (x[:n_p].reshape(x_prompt.shape), x[n_p:].reshape(x_sample.shape))
```

```python
import functools
import math

import jax
import jax.numpy as jnp
from jax import lax
from jax.experimental import pallas as pl
from jax.experimental.pallas import tpu as pltpu

F32 = jnp.float32
BF16 = jnp.bfloat16

D_MODEL = 1024
GRID_W = 64
ROPE_THETA = 10000.0
RMS_EPS = 1e-6
LN_EPS = 1e-5
NEG_INF = -1e30

MLA_HEADS = 4
MLA_Q_RANK = 384
MLA_KV_RANK = 256
MLA_NOPE = 128
MLA_ROPE = 64
MLA_V = 128
MLA_QK = MLA_NOPE + MLA_ROPE

HEADS = 8
KV_HEADS = 2
HEAD_DIM = 64
GROUP = HEADS // KV_HEADS
WINDOW = 128
Q_BLOCK = 128
SPAN = Q_BLOCK + 2 * WINDOW

N_BRANCH = 3
N_EXPERTS = 8
TOP_K = 2

LANES = 128
BF16_ROWS = 16
MXU_DIM = 256
VMEM_LIMIT = 56 * 1024 * 1024

MLA_QK_PAD = MXU_DIM
QW = HEADS * HEAD_DIM
KW = KV_HEADS * HEAD_DIM
C_CQ = 0
C_CKV = C_CQ + MLA_Q_RANK
C_KR = C_CKV + MLA_KV_RANK
C_WQ = C_KR + LANES
C_WK = C_WQ + QW
C_WV = C_WK + KW
C_AQ = C_WV + KW
C_AK = C_AQ + QW
C_AV = C_AK + KW
C_END = C_AV + KW
TAB_W = 6 * LANES
LOG2E = math.log2(math.e)
ATT_TK = 256

HEAD_PERM = [h for c in range(GROUP) for h in (c, GROUP + c)]


def _cparams(sem, vmem=VMEM_LIMIT, flags=None):
    return pltpu.CompilerParams(dimension_semantics=sem, vmem_limit_bytes=vmem, flags=flags)


ATT_FLAGS = None


def _pick_tile(n, pref):
    t = min(pref, n)
    while n % t:
        t //= 2
    return t


def _ln_rows(y, g, b):
    mu = jnp.mean(y, axis=-1, keepdims=True)
    d = y - mu
    var = jnp.mean(d * d, axis=-1, keepdims=True)
    return d * lax.rsqrt(var + LN_EPS) * g + b


def _rms_rows(x, g):
    ms = jnp.mean(x * x, axis=-1, keepdims=True)
    return x * lax.rsqrt(ms + RMS_EPS) * g


def _rope_chunk(x, cos, sin_a, sin_b, half):
    return (x * cos + pltpu.roll(x, LANES - half, 1) * sin_a + pltpu.roll(x, half, 1) * sin_b)


def _dot(a, b):
    return jnp.dot(a, b, preferred_element_type=F32)


def _ln_kernel(x_ref, g_ref, b_ref, o_ref):
    o_ref[...] = _ln_rows(x_ref[...], g_ref[...], b_ref[...])


def _embed_ln(x, g, b, tm):
    n = x.shape[0]
    return pl.pallas_call(
        _ln_kernel,
        grid=(n // tm,),
        in_specs=[pl.BlockSpec((tm, D_MODEL), lambda i: (i, 0)),
                  pl.BlockSpec((1, D_MODEL), lambda i: (0, 0)),
                  pl.BlockSpec((1, D_MODEL), lambda i: (0, 0))],
        out_specs=pl.BlockSpec((tm, D_MODEL), lambda i: (i, 0)),
        out_shape=jax.ShapeDtypeStruct((n, D_MODEL), F32),
        compiler_params=_cparams(("parallel",)),
        name="embed_ln",
    )(x, g.reshape(1, -1), b.reshape(1, -1))


def _proj_kernel(pos_ref, x_ref, ws_ref, wuq_ref, wukv_ref, gq_ref, gkv_ref, gaq_ref, gak_ref,
                 bd_ref, tab_ref,
                 qm_ref, km_ref, vm_ref, wq_ref, wk_ref, wv_ref, aq_ref, ak_ref, av_ref):
    del pos_ref
    tm = x_ref.shape[0]
    x = x_ref[...].astype(BF16)
    res = _dot(x, ws_ref[...])

    m_cos = tab_ref[:, 0 * LANES:1 * LANES]
    m_sa = tab_ref[:, 1 * LANES:2 * LANES]
    m_sb = tab_ref[:, 2 * LANES:3 * LANES]
    a_cos = tab_ref[:, 3 * LANES:4 * LANES]
    a_sa = tab_ref[:, 4 * LANES:5 * LANES]
    a_sb = tab_ref[:, 5 * LANES:6 * LANES]
    c_q = _rms_rows(res[:, C_CQ:C_CKV], gq_ref[...]).astype(BF16)
    q = _dot(c_q, wuq_ref[...])
    q_scale = MLA_QK ** -0.5 * LOG2E
    for h in range(MLA_HEADS):
        base = h * MLA_QK_PAD
        qm_ref[:, base:base + LANES] = (q[:, base:base + LANES] * q_scale).astype(BF16)
        roped = _rope_chunk(q[:, base + LANES:base + 2 * LANES], m_cos, m_sa, m_sb, MLA_ROPE // 2)
        qm_ref[:, base + LANES:base + 2 * LANES] = (roped * q_scale).astype(BF16)

    c_kv = _rms_rows(res[:, C_CKV:C_KR], gkv_ref[...]).astype(BF16)
    kv = _dot(c_kv, wukv_ref[...])
    k_rope = _rope_chunk(res[:, C_KR:C_WQ], m_cos, m_sa, m_sb, MLA_ROPE // 2).astype(BF16)
    for h in range(MLA_HEADS):
        base = h * MLA_QK_PAD
        km_ref[:, base:base + LANES] = kv[:, h * LANES:(h + 1) * LANES].astype(BF16)
        km_ref[:, base + LANES:base + 2 * LANES] = k_rope
    v_off = MLA_HEADS * MLA_NOPE
    for r in range(tm // ATT_TK):
        rows = slice(r * ATT_TK, (r + 1) * ATT_TK)
        for h in range(MLA_HEADS):
            vm_ref[r, h * MLA_V:(h + 1) * MLA_V, :] = (
                kv[rows, v_off + h * MLA_V:v_off + (h + 1) * MLA_V].T.astype(BF16))
        av_ref[r] = res[rows, C_AV:C_END].T.astype(BF16)

    wq_ref[...] = (res[:, C_WQ:C_WK] * (HEAD_DIM ** -0.5 * LOG2E)).astype(BF16)
    wk_ref[...] = res[:, C_WK:C_WV].astype(BF16)
    for r in range(tm // Q_BLOCK):
        wv_ref[r] = res[r * Q_BLOCK:(r + 1) * Q_BLOCK, C_WV:C_AQ].T.astype(BF16)

    bd = bd_ref[...]

    def norm_rope(xc, g):
        ms = _dot((xc * xc).astype(BF16), bd)
        xn = xc * lax.rsqrt(ms + RMS_EPS) * g
        return _rope_chunk(xn, a_cos, a_sa, a_sb, HEAD_DIM // 4)

    for c in range(QW // LANES):
        xc = res[:, C_AQ + c * LANES:C_AQ + (c + 1) * LANES]
        aq_ref[:, c * LANES:(c + 1) * LANES] = (
            norm_rope(xc, gaq_ref[...]) * (HEAD_DIM ** -0.5 * LOG2E)).astype(BF16)
    ak_ref[...] = norm_rope(res[:, C_AK:C_AV], gak_ref[...]).astype(BF16)


def _in_proj(x, pos_blk, lw, tab, tm):
    n = x.shape[0]
    const = lambda shape: pl.BlockSpec(shape, lambda i, p: (0, 0))
    rows = lambda w: pl.BlockSpec((tm, w), lambda i, p: (i, 0))
    out_widths = [MLA_HEADS * MLA_QK_PAD, MLA_HEADS * MLA_QK_PAD, (MLA_HEADS * MLA_V, ATT_TK),
                  QW, KW, (KW, Q_BLOCK), QW, KW, (KW, ATT_TK)]

    def out_spec(w):
        if isinstance(w, tuple):
            return pl.BlockSpec((tm // w[1], w[0], w[1]), lambda i, p: (i, 0, 0))
        return rows(w)

    def out_shape(w):
        if isinstance(w, tuple):
            return jax.ShapeDtypeStruct((n // w[1], w[0], w[1]), BF16)
        return jax.ShapeDtypeStruct((n, w), BF16)

    grid_spec = pltpu.PrefetchScalarGridSpec(
        num_scalar_prefetch=1,
        grid=(n // tm,),
        in_specs=[rows(D_MODEL),
                  const((D_MODEL, C_END)),
                  const((MLA_Q_RANK, MLA_HEADS * MLA_QK_PAD)),
                  const((MLA_KV_RANK, 2 * MLA_HEADS * LANES)),
                  const((1, MLA_Q_RANK)), const((1, MLA_KV_RANK)),
                  const((1, LANES)), const((1, LANES)),
                  const((LANES, LANES)),
                  pl.BlockSpec((tm, TAB_W), lambda i, p: (p[i], 0))],
        out_specs=[out_spec(w) for w in out_widths],
    )
    return pl.pallas_call(
        _proj_kernel,
        grid_spec=grid_spec,
        out_shape=[out_shape(w) for w in out_widths],
        compiler_params=_cparams(("parallel",)),
        name="in_proj",
    )(pos_blk, x, lw["w_s"], lw["w_uq"], lw["w_ukv"], lw["g_q"], lw["g_kv"], lw["g_aq"], lw["g_ak"],
      lw["bd"], tab)


def _call_into(prev, kernel, operands, *, in_specs, **kwargs):
    if prev is None:
        prev = jnp.zeros(kwargs["out_shape"].shape, kwargs["out_shape"].dtype)
    n_in = len(operands)

    def skip_prev(*refs):
        return kernel(*refs[:n_in], *refs[n_in + 1:])

    return pl.pallas_call(skip_prev, in_specs=list(in_specs) + [pl.BlockSpec(memory_space=pl.ANY)],
                          input_output_aliases={n_in: 0}, **kwargs)(*operands, prev)


def _flash_t(qt_sc, k_ref, vt_ref, s0_sc, s1_sc, acc_sc, seq, v_rows):
    n_blk = acc_sc.shape[0]
    nk = seq // ATT_TK
    assert nk >= 2 and nk % 2 == 0
    acc_sc[...] = jnp.zeros(acc_sc.shape, F32)
    stat = lambda v: tuple(jnp.full((1, MXU_DIM), v, F32) for _ in range(n_blk))

    def scores(j, s_ref):
        k = k_ref[pl.ds(pl.multiple_of(j * ATT_TK, ATT_TK), ATT_TK), :]
        tops = []
        for b in range(n_blk):
            s = _dot(k, qt_sc[:, b * MXU_DIM:(b + 1) * MXU_DIM])
            s_ref[b] = s
            tops.append(jnp.max(s, axis=0, keepdims=True))
        return tuple(tops)

    ones_rows = jnp.where(lax.broadcasted_iota(jnp.int32, (BF16_ROWS, ATT_TK), 0) == 0, 1.0, 0.0).astype(BF16)

    def consume(j, s_ref, tops, ms):
        ms = list(ms)
        for b in range(n_blk):
            m_new = jnp.maximum(ms[b], tops[b])
            alpha = jnp.exp2(ms[b] - m_new)
            p = jnp.exp2((s_ref[b] - m_new).astype(BF16))
            ms[b] = m_new
            r0, r1 = v_rows[b]
            vt = jnp.concatenate([vt_ref[j, r0:r1, :], ones_rows], axis=0)
            acc_sc[b] = acc_sc[b] * alpha + _dot(vt, p)
        return tuple(ms)

    def pair(i, carry):
        ms, tops0 = carry
        tops1 = scores(2 * i + 1, s1_sc)
        ms = consume(2 * i, s0_sc, tops0, ms)
        tops0 = scores(2 * i + 2, s0_sc)
        return consume(2 * i + 1, s1_sc, tops1, ms), tops0

    trips = nk // 2 - 1
    unroll = trips if trips <= 5 else next(u for u in (5, 4, 3, 2, 1) if trips % u == 0)
    ms, tops0 = lax.fori_loop(0, trips, pair, (stat(NEG_INF), scores(0, s0_sc)), unroll=max(unroll, 1))
    tops1 = scores(nk - 1, s1_sc)
    ms = consume(nk - 2, s0_sc, tops0, ms)
    consume(nk - 1, s1_sc, tops1, ms)


def _stage_queries(q_ref, qt_sc):
    tq = q_ref.shape[0]
    lo = lax.broadcasted_iota(jnp.int32, (LANES, tq), 0) < HEAD_DIM
    for c in range(GROUP):
        qt = q_ref[:, c * LANES:(c + 1) * LANES].astype(F32).T
        qt_sc[:, c * tq:(c + 1) * tq] = jnp.where(lo, qt, 0.0).astype(BF16)
        qt_sc[:, (GROUP + c) * tq:(GROUP + c + 1) * tq] = jnp.where(lo, 0.0, qt).astype(BF16)


def _store_heads(head_out, o_ref):
    for c in range(GROUP):
        ot = jnp.concatenate([head_out(c), head_out(GROUP + c)], axis=0)
        o_ref[:, c * LANES:(c + 1) * LANES] = ot.T.astype(o_ref.dtype)


AXIAL_TQ = 128
AXIAL_SWEEPS = 4


def _gqa_kernel(q_ref, k_ref, vt_ref, o_ref, *scratch, seq):
    tq = AXIAL_TQ
    heads_per_blk = MXU_DIM // tq
    for t in range(AXIAL_SWEEPS):
        qt_sc, s0_sc, s1_sc, acc_sc = scratch[4 * t:4 * t + 4]
        rows = pl.ds(t * tq, tq)
        _stage_queries(q_ref.at[rows], qt_sc)
        v_rows = [((b * heads_per_blk) // GROUP * HEAD_DIM, ((b * heads_per_blk) // GROUP + 1) * HEAD_DIM)
                  for b in range(acc_sc.shape[0])]
        _flash_t(qt_sc, k_ref, vt_ref, s0_sc, s1_sc, acc_sc, seq, v_rows)

        def head_out(h, acc_sc=acc_sc):
            b, off = divmod(h * tq, MXU_DIM)
            return acc_sc[b, 0:HEAD_DIM, off:off + tq] / acc_sc[b, HEAD_DIM:HEAD_DIM + 1, off:off + tq]

        _store_heads(head_out, o_ref.at[rows])


def _axial_attention(q, k, vt, prev, tok_off, batch, seq):
    tq = AXIAL_TQ * AXIAL_SWEEPS
    nq = seq // tq
    nkb = seq // ATT_TK
    qb, sb = tok_off // tq, tok_off // seq
    n_blk = HEADS * AXIAL_TQ // MXU_DIM
    sweep_scratch = [pltpu.VMEM((KW, HEADS * AXIAL_TQ), BF16),
                     pltpu.VMEM((n_blk, ATT_TK, MXU_DIM), F32),
                     pltpu.VMEM((n_blk, ATT_TK, MXU_DIM), F32),
                     pltpu.VMEM((n_blk, HEAD_DIM + BF16_ROWS, MXU_DIM), F32)]
    return _call_into(
        prev, functools.partial(_gqa_kernel, seq=seq), (q, k, vt),
        grid=(batch, nq),
        in_specs=[pl.BlockSpec((tq, QW), lambda b, i: (qb + b * nq + i, 0)),
                  pl.BlockSpec((seq, KW), lambda b, i: (sb + b, 0)),
                  pl.BlockSpec((nkb, KW, ATT_TK), lambda b, i: (sb + b, 0, 0))],
        out_specs=pl.BlockSpec((tq, QW), lambda b, i: (qb + b * nq + i, 0)),
        out_shape=jax.ShapeDtypeStruct((q.shape[0], QW), BF16),
        scratch_shapes=sweep_scratch * AXIAL_SWEEPS,
        compiler_params=_cparams(("parallel", "arbitrary"), flags=ATT_FLAGS),
        name="axial_attention",
    )


MLA_SWEEP_BLOCKS = 4
MLA_SWEEPS = 2


def _mla_kernel(q_ref, k_ref, vt_ref, o_ref, *scratch, seq):
    sweeps = len(scratch) // 4
    n_blk = q_ref.shape[0] // (sweeps * MXU_DIM)
    for t in range(sweeps):
        qt_sc, s0_sc, s1_sc, acc_sc = scratch[4 * t:4 * t + 4]
        for b in range(n_blk):
            rows = pl.ds((t * n_blk + b) * MXU_DIM, MXU_DIM)
            qt_sc[:, b * MXU_DIM:(b + 1) * MXU_DIM] = q_ref[rows, :].astype(F32).T.astype(BF16)
        _flash_t(qt_sc, k_ref, vt_ref, s0_sc, s1_sc, acc_sc, seq, [(0, MLA_V)] * n_blk)
        for b in range(n_blk):
            rows = pl.ds((t * n_blk + b) * MXU_DIM, MXU_DIM)
            o = acc_sc[b, 0:MLA_V, :] / acc_sc[b, MLA_V:MLA_V + 1, :]
            o_ref[rows, :] = o.T.astype(o_ref.dtype)


def _mla_attention(q, k, vt, prev, tok_off, batch, seq):
    tq = _pick_tile(seq, MLA_SWEEPS * MLA_SWEEP_BLOCKS * MXU_DIM)
    sweeps = MLA_SWEEPS if tq % (MLA_SWEEPS * MXU_DIM) == 0 else 1
    n_blk = tq // (sweeps * MXU_DIM)
    assert n_blk * sweeps * MXU_DIM == tq
    nq = seq // tq
    nkb = seq // ATT_TK
    qb, sb = tok_off // tq, tok_off // seq
    sweep_scratch = [pltpu.VMEM((MLA_QK_PAD, n_blk * MXU_DIM), BF16),
                     pltpu.VMEM((n_blk, ATT_TK, MXU_DIM), F32),
                     pltpu.VMEM((n_blk, ATT_TK, MXU_DIM), F32),
                     pltpu.VMEM((n_blk, MLA_V + BF16_ROWS, MXU_DIM), F32)]
    return _call_into(
        prev, functools.partial(_mla_kernel, seq=seq), (q, k, vt),
        grid=(batch, MLA_HEADS, nq),
        in_specs=[pl.BlockSpec((tq, MLA_QK_PAD), lambda b, h, i: (qb + b * nq + i, h)),
                  pl.BlockSpec((seq, MLA_QK_PAD), lambda b, h, i: (sb + b, h)),
                  pl.BlockSpec((nkb, MLA_V, ATT_TK), lambda b, h, i: (sb + b, h, 0))],
        out_specs=pl.BlockSpec((tq, MLA_V), lambda b, h, i: (qb + b * nq + i, h)),
        out_shape=jax.ShapeDtypeStruct((q.shape[0], MLA_HEADS * MLA_V), BF16),
        scratch_shapes=sweep_scratch * sweeps,
        compiler_params=_cparams(("parallel", "parallel", "arbitrary"), flags=ATT_FLAGS),
        name="mla_attention",
    )


WINDOW_SWEEPS = 4


def _window_kernel(sink_ref, q_ref, k_ref, vt_ref, o_ref, *qt_scs, seq):
    for t, qt_sc in enumerate(qt_scs):
        rows = pl.ds(t * Q_BLOCK, Q_BLOCK)
        _window_block(sink_ref, q_ref.at[rows], k_ref, vt_ref, o_ref.at[rows], qt_sc,
                      (pl.program_id(1) * len(qt_scs) + t) * Q_BLOCK, seq)


def _window_block(sink_ref, q_ref, k_ref, vt_ref, o_ref, qt_sc, start, seq):
    kstart = pl.multiple_of(jnp.clip(start - WINDOW, 0, seq - SPAN), Q_BLOCK)
    kb = kstart // Q_BLOCK
    _stage_queries(q_ref, qt_sc)
    s_all = _dot(k_ref[pl.ds(kstart, SPAN), :], qt_sc[...])
    vt = jnp.concatenate([vt_ref[kb + j] for j in range(SPAN // Q_BLOCK)], axis=1)
    ones_rows = jnp.where(lax.broadcasted_iota(jnp.int32, (BF16_ROWS, SPAN), 0) == 0, 1.0, 0.0).astype(BF16)
    rel = ((kstart + lax.broadcasted_iota(jnp.int32, (SPAN, Q_BLOCK), 0))
           - (start + lax.broadcasted_iota(jnp.int32, (SPAN, Q_BLOCK), 1)))
    dist_i = jnp.abs(rel)
    dist = jnp.where(dist_i <= WINDOW, dist_i.astype(F32), -NEG_INF * 2.0 ** HEADS)
    outs = []
    for blk in range(HEADS // 2):
        ps, sinks = [], []
        for h in (2 * blk, 2 * blk + 1):
            slope = 2.0 ** (-8.0 * (h + 1) / HEADS) * LOG2E
            sink = sink_ref[h] * LOG2E
            logits = s_all[:, h * Q_BLOCK:(h + 1) * Q_BLOCK] - slope * dist
            m = jnp.maximum(jnp.max(logits, axis=0, keepdims=True), sink)
            ps.append(jnp.exp2((logits - m).astype(BF16)))
            sinks.append(jnp.exp2(sink - m))
        g = (2 * blk) // GROUP
        lhs = jnp.concatenate([vt[g * HEAD_DIM:(g + 1) * HEAD_DIM], ones_rows], axis=0)
        pv = _dot(lhs, jnp.concatenate(ps, axis=1))
        denom = pv[HEAD_DIM:HEAD_DIM + 1] + jnp.concatenate(sinks, axis=1)
        outs.append(pv[0:HEAD_DIM] / denom)
    _store_heads(lambda h: outs[h // 2][:, (h % 2) * Q_BLOCK:(h % 2 + 1) * Q_BLOCK], o_ref)


def _window_attention(q, k, v, sink, prev, tok_off, batch, seq):
    tq = WINDOW_SWEEPS * Q_BLOCK
    assert seq >= SPAN and seq % tq == 0
    nq = seq // tq
    qb, sb = tok_off // tq, tok_off // seq
    return _call_into(
        prev, functools.partial(_window_kernel, seq=seq), (sink, q, k, v),
        grid=(batch, nq),
        in_specs=[pl.BlockSpec(memory_space=pltpu.SMEM),
                  pl.BlockSpec((tq, QW), lambda b, i: (qb + b * nq + i, 0)),
                  pl.BlockSpec((seq, KW), lambda b, i: (sb + b, 0)),
                  pl.BlockSpec((seq // Q_BLOCK, KW, Q_BLOCK), lambda b, i: (sb + b, 0, 0))],
        out_specs=pl.BlockSpec((tq, QW), lambda b, i: (qb + b * nq + i, 0)),
        out_shape=jax.ShapeDtypeStruct((q.shape[0], QW), BF16),
        scratch_shapes=[pltpu.VMEM((KW, HEADS * Q_BLOCK), BF16)] * WINDOW_SWEEPS,
        compiler_params=_cparams(("parallel", "arbitrary")),
        name="window_attention",
    )


def _route(x, w_hi, w_lo):
    hi = x.astype(BF16)
    lo = (x - hi.astype(F32)).astype(BF16)
    logits = _dot(hi, w_hi) + _dot(lo, w_hi) + _dot(hi, w_lo)
    lane = lax.broadcasted_iota(jnp.int32, logits.shape, 1)
    logits = jnp.where(lane < N_EXPERTS, logits, NEG_INF)
    t1 = jnp.max(logits, axis=1, keepdims=True)
    e1 = jnp.min(jnp.where(logits == t1, lane, LANES), axis=1, keepdims=True)
    rest = jnp.where(lane == e1, NEG_INF, logits)
    t2 = jnp.max(rest, axis=1, keepdims=True)
    e2 = jnp.min(jnp.where(rest == t2, lane, LANES), axis=1, keepdims=True)
    w = jnp.exp(t2 - t1)
    g1 = 1.0 / (1.0 + w)
    g2 = w / (1.0 + w)
    return (jnp.where(lane == 0, e1, jnp.where(lane == 1, e2, 0)),
            jnp.where(lane == 0, g1, jnp.where(lane == 1, g2, 0.0)))


def _router_kernel(x_ref, whi_ref, wlo_ref, e_ref, gate_ref):
    e_ref[...], gate_ref[...] = _route(x_ref[...], whi_ref[...], wlo_ref[...])


def _router(x, w_hi, w_lo, tm):
    n = x.shape[0]
    return pl.pallas_call(
        _router_kernel,
        grid=(n // tm,),
        in_specs=[pl.BlockSpec((tm, D_MODEL), lambda i: (i, 0)),
                  pl.BlockSpec((D_MODEL, LANES), lambda i: (0, 0)),
                  pl.BlockSpec((D_MODEL, LANES), lambda i: (0, 0))],
        out_specs=[pl.BlockSpec((tm, LANES), lambda i: (i, 0)),
                   pl.BlockSpec((tm, LANES), lambda i: (i, 0))],
        out_shape=[jax.ShapeDtypeStruct((n, LANES), jnp.int32),
                   jax.ShapeDtypeStruct((n, LANES), F32)],
        compiler_params=_cparams(("parallel",)),
        name="moe_router",
    )(x, w_hi, w_lo)


def _merge_kernel(x_ref, oa_ref, ob_ref, oc_ref, wg_ref, wa_ref, wb_ref, wc_ref, wo_ref, g_ref, b_ref,
                  o_ref, *, alpha):
    x = x_ref[...]
    xb = x.astype(BF16)
    merged = None
    for idx, (o_br, w_br) in enumerate(((oa_ref, wa_ref), (ob_ref, wb_ref), (oc_ref, wc_ref))):
        gate = jax.nn.sigmoid(_dot(xb, wg_ref[:, idx * D_MODEL:(idx + 1) * D_MODEL]))
        term = gate * _dot(o_br[...], w_br[...])
        merged = term if merged is None else merged + term
    m = _dot(merged.astype(BF16), wo_ref[...])
    o_ref[...] = _ln_rows(alpha * x + m, g_ref[...], b_ref[...])


def _merge(x, o_a, o_b, o_c, lw, alpha, tm):
    n = x.shape[0]
    const = lambda shape: pl.BlockSpec(shape, lambda i: (0, 0))
    rows = lambda w: pl.BlockSpec((tm, w), lambda i: (i, 0))
    return pl.pallas_call(
        functools.partial(_merge_kernel, alpha=alpha),
        grid=(n // tm,),
        in_specs=[rows(D_MODEL), rows(MLA_HEADS * MLA_V), rows(QW), rows(QW),
                  const((D_MODEL, N_BRANCH * D_MODEL)),
                  const((MLA_HEADS * MLA_V, D_MODEL)), const((QW, D_MODEL)), const((QW, D_MODEL)),
                  const((D_MODEL, D_MODEL)), const((1, D_MODEL)), const((1, D_MODEL))],
        out_specs=rows(D_MODEL),
        out_shape=jax.ShapeDtypeStruct((n, D_MODEL), F32),
        compiler_params=_cparams(("parallel",)),
        name="gated_merge",
    )(x, o_a, o_b, o_c, lw["w_g"], lw["w_br_mla"], lw["w_br_win"], lw["w_br_ax"], lw["w_out"],
      lw["ln1_g"], lw["ln1_b"])


def _ffn_kernel(x_ref, w1_ref, w3_ref, w2_ref, g_ref, b_ref, o_ref, acc_ref, *, alpha):
    f = pl.program_id(1)

    @pl.when(f == 0)
    def _():
        acc_ref[...] = jnp.zeros(acc_ref.shape, F32)

    xb = x_ref[...].astype(BF16)
    h = jax.nn.silu(_dot(xb, w1_ref[...])) * _dot(xb, w3_ref[...])
    acc_ref[...] += _dot(h.astype(BF16), w2_ref[...])

    @pl.when(f == pl.num_programs(1) - 1)
    def _():
        o_ref[...] = _ln_rows(alpha * x_ref[...] + acc_ref[...], g_ref[...], b_ref[...])


def _dense_ffn(x, w1, w3, w2, g, b, alpha, tm, tf):
    n = x.shape[0]
    d_ff = w1.shape[1]
    mode = dict(pipeline_mode=pl.Buffered(1)) if tf == d_ff else {}
    return pl.pallas_call(
        functools.partial(_ffn_kernel, alpha=alpha),
        grid=(n // tm, d_ff // tf),
        in_specs=[pl.BlockSpec((tm, D_MODEL), lambda i, f: (i, 0)),
                  pl.BlockSpec((D_MODEL, tf), lambda i, f: (0, f), **mode),
                  pl.BlockSpec((D_MODEL, tf), lambda i, f: (0, f), **mode),
                  pl.BlockSpec((tf, D_MODEL), lambda i, f: (f, 0), **mode),
                  pl.BlockSpec((1, D_MODEL), lambda i, f: (0, 0)),
                  pl.BlockSpec((1, D_MODEL), lambda i, f: (0, 0))],
        out_specs=pl.BlockSpec((tm, D_MODEL), lambda i, f: (i, 0)),
        out_shape=jax.ShapeDtypeStruct((n, D_MODEL), F32),
        scratch_shapes=[pltpu.VMEM((tm, D_MODEL), F32)],
        compiler_params=_cparams(("parallel", "arbitrary")),
        name="dense_ffn",
    )(x, w1, w3, w2, g, b)


ROW_TILE = 8


def _row_of(ref3, r):
    return ref3.at[lax.shift_right_logical(r, 3), pl.ds(jnp.bitwise_and(r, ROW_TILE - 1), 1)]


def _dispatch_kernel(pad_ref, dest_hbm, x_ref, xs_hbm, dest_smem, zero_ref, idx_sem, row_sem, *, n_pad, zpad):
    i = pl.program_id(0)
    groups = x_ref.shape[0]
    n_idx = TOP_K * ROW_TILE * groups
    cp = pltpu.make_async_copy(dest_hbm.at[pl.ds(pl.multiple_of(i * n_idx, n_idx), n_idx)], dest_smem, idx_sem)
    cp.start()
    zero_ref[...] = jnp.zeros(zero_ref.shape, F32)
    cp.wait()

    def send(g, carry):
        for u in range(ROW_TILE):
            for k in range(TOP_K):
                d = dest_smem[(g * ROW_TILE + u) * TOP_K + k]
                pltpu.make_async_copy(x_ref.at[g, pl.ds(u, 1)], _row_of(xs_hbm, d), row_sem).start(priority=k)
        return carry

    lax.fori_loop(0, groups, send, 0)

    def pad_copy(r):
        return pltpu.make_async_copy(zero_ref.at[pl.ds(0, 1)], _row_of(xs_hbm, pad_ref[i * zpad + r]), row_sem)

    def send_zero(r, carry):
        @pl.when(i * zpad + r < n_pad)
        def _():
            pad_copy(r).start()
        return carry

    def drain_zero(r, carry):
        @pl.when(i * zpad + r < n_pad)
        def _():
            pad_copy(r).wait()
        return carry

    def drain(r, carry):
        pltpu.make_async_copy(x_ref.at[0, pl.ds(0, 1)], xs_hbm.at[0, pl.ds(0, 1)], row_sem).wait()
        return carry

    lax.fori_loop(0, zpad, send_zero, 0)
    lax.fori_loop(0, n_idx, drain, 0, unroll=8)
    lax.fori_loop(0, zpad, drain_zero, 0)


def _dispatch_rows(x, dest, pad_slots, p, tm):
    n = x.shape[0]
    steps = n // tm
    n_pad = pad_slots.shape[0]
    zpad = -(-n_pad // steps)
    grid_spec = pltpu.PrefetchScalarGridSpec(
        num_scalar_prefetch=1,
        grid=(steps,),
        in_specs=[pl.BlockSpec(memory_space=pl.ANY),
                  pl.BlockSpec((tm // ROW_TILE, ROW_TILE, D_MODEL), lambda i, pad: (i, 0, 0))],
        out_specs=pl.BlockSpec(memory_space=pl.ANY),
        scratch_shapes=[pltpu.SMEM((TOP_K * tm,), jnp.int32),
                        pltpu.VMEM((ROW_TILE, D_MODEL), F32),
                        pltpu.SemaphoreType.DMA, pltpu.SemaphoreType.DMA],
    )
    xs = pl.pallas_call(
        functools.partial(_dispatch_kernel, n_pad=n_pad, zpad=zpad),
        grid_spec=grid_spec,
        out_shape=jax.ShapeDtypeStruct((p // ROW_TILE, ROW_TILE, D_MODEL), F32),
        compiler_params=_cparams(("arbitrary",)),
        name="moe_dispatch",
    )(pad_slots, dest, x.reshape(n // ROW_TILE, ROW_TILE, D_MODEL))
    return xs.reshape(p, D_MODEL)


def _expert_kernel(be_ref, bv_ref, x_ref, w1_ref, w3_ref, w2_ref, o_ref, acc_ref):
    del be_ref
    i = pl.program_id(0)
    f = pl.program_id(1)
    valid = bv_ref[i] != 0

    @pl.when(f == 0)
    def _():
        acc_ref[...] = jnp.zeros(acc_ref.shape, F32)

    @pl.when(valid)
    def _():
        xb = x_ref[...].astype(BF16)
        h = jax.nn.silu(_dot(xb, w1_ref[0])) * _dot(xb, w3_ref[0])
        acc_ref[...] += _dot(h.astype(BF16), w2_ref[0])

    @pl.when(f == pl.num_programs(1) - 1)
    def _():
        o_ref[...] = acc_ref[...]


def _expert_ffn(xs, blk_expert, blk_valid, w1, w3, w2, tm, tf):
    p = xs.shape[0]
    d_ff = w1.shape[2]
    nf = d_ff // tf
    fidx = lambda i, f, be, bv: jnp.where(bv[i] != 0, f, nf - 1)
    grid_spec = pltpu.PrefetchScalarGridSpec(
        num_scalar_prefetch=2,
        grid=(p // tm, nf),
        in_specs=[pl.BlockSpec((tm, D_MODEL), lambda i, f, be, bv: (i, 0)),
                  pl.BlockSpec((1, D_MODEL, tf), lambda i, f, be, bv: (be[i], 0, fidx(i, f, be, bv))),
                  pl.BlockSpec((1, D_MODEL, tf), lambda i, f, be, bv: (be[i], 0, fidx(i, f, be, bv))),
                  pl.BlockSpec((1, tf, D_MODEL), lambda i, f, be, bv: (be[i], fidx(i, f, be, bv), 0))],
        out_specs=pl.BlockSpec((tm, D_MODEL), lambda i, f, be, bv: (i, 0)),
        scratch_shapes=[pltpu.VMEM((tm, D_MODEL), F32)],
    )
    return pl.pallas_call(
        _expert_kernel,
        grid_spec=grid_spec,
        out_shape=jax.ShapeDtypeStruct((p, D_MODEL), F32),
        compiler_params=_cparams(("arbitrary", "arbitrary")),
        name="moe_experts",
    )(blk_expert, blk_valid, xs, w1, w3, w2)


def _combine_kernel(pos_hbm, ys_hbm, x_ref, gate_ref, g_ref, b_ref, o_ref,
                    r0_ref, r1_ref, pos_smem, pos_sem, row_sem, *, alpha):
    i = pl.program_id(0)
    tm = x_ref.shape[0]
    n_idx = TOP_K * tm
    cp = pltpu.make_async_copy(pos_hbm.at[pl.ds(pl.multiple_of(i * n_idx, n_idx), n_idx)], pos_smem, pos_sem)
    cp.start()
    cp.wait()

    def fetch(g, carry):
        for u in range(ROW_TILE):
            for k, buf in enumerate((r0_ref, r1_ref)):
                src = _row_of(ys_hbm, pos_smem[(g * ROW_TILE + u) * TOP_K + k])
                pltpu.make_async_copy(src, buf.at[g, pl.ds(u, 1)], row_sem).start(priority=k)
        return carry

    def drain(r, carry):
        pltpu.make_async_copy(ys_hbm.at[0, pl.ds(0, 1)], r0_ref.at[0, pl.ds(0, 1)], row_sem).wait()
        return carry

    lax.fori_loop(0, tm // ROW_TILE, fetch, 0)
    lax.fori_loop(0, n_idx, drain, 0, unroll=8)
    gates = gate_ref[...]
    y = (gates[:, 0:1] * r0_ref[...].reshape(tm, D_MODEL) + gates[:, 1:2] * r1_ref[...].reshape(tm, D_MODEL))
    o_ref[...] = _ln_rows(alpha * x_ref[...] + y, g_ref[...], b_ref[...])


def _moe_combine(x, ys, pos, gates, g, b, alpha, tm):
    n = x.shape[0]
    return pl.pallas_call(
        functools.partial(_combine_kernel, alpha=alpha),
        grid=(n // tm,),
        in_specs=[pl.BlockSpec(memory_space=pl.ANY), pl.BlockSpec(memory_space=pl.ANY),
                  pl.BlockSpec((tm, D_MODEL), lambda i: (i, 0)),
                  pl.BlockSpec((tm, LANES), lambda i: (i, 0)),
                  pl.BlockSpec((1, D_MODEL), lambda i: (0, 0)),
                  pl.BlockSpec((1, D_MODEL), lambda i: (0, 0))],
        out_specs=pl.BlockSpec((tm, D_MODEL), lambda i: (i, 0)),
        out_shape=jax.ShapeDtypeStruct((n, D_MODEL), F32),
        scratch_shapes=[pltpu.VMEM((tm // ROW_TILE, ROW_TILE, D_MODEL), F32),
                        pltpu.VMEM((tm // ROW_TILE, ROW_TILE, D_MODEL), F32),
                        pltpu.SMEM((TOP_K * tm,), jnp.int32),
                        pltpu.SemaphoreType.DMA, pltpu.SemaphoreType.DMA],
        compiler_params=_cparams(("arbitrary",)),
        name="moe_combine",
    )(pos, ys.reshape(ys.shape[0] // ROW_TILE, ROW_TILE, D_MODEL), x, gates, g, b)


def _moe_layer(x, e_out, gates, w1, w3, w2, g, b, alpha, tm, tm_e, tf):
    n = x.shape[0]
    a = n * TOP_K
    e_flat = e_out[:, :TOP_K].reshape(a)
    onehot = (e_flat[:, None] == jnp.arange(N_EXPERTS, dtype=jnp.int32)[None, :]).astype(jnp.int32)
    csum = jnp.cumsum(onehot, axis=0)
    counts = csum[-1]
    rank = jnp.sum((csum - 1) * onehot, axis=1)
    padded = (counts + tm_e - 1) // tm_e * tm_e
    pad_end = jnp.cumsum(padded)
    pad_start = pad_end - padded
    dest = (pad_start[e_flat] + rank).astype(jnp.int32)
    assert a % tm_e == 0
    n_pad = N_EXPERTS * tm_e
    p = a + n_pad
    npad = padded - counts
    pad_cum = jnp.cumsum(npad)
    pad_i = jnp.arange(n_pad, dtype=jnp.int32)
    pad_e = jnp.searchsorted(pad_cum, pad_i, side="right")
    pad_ec = jnp.minimum(pad_e, N_EXPERTS - 1)
    pad_slots = jnp.where(pad_e >= N_EXPERTS, pad_end[-1] + (pad_i - pad_cum[-1]),
                          (pad_start + counts)[pad_ec] + (pad_i - (pad_cum - npad)[pad_ec])).astype(jnp.int32)
    blk_start = jnp.arange(p // tm_e, dtype=jnp.int32) * tm_e
    blk_valid = (blk_start < pad_end[-1]).astype(jnp.int32)
    blk_expert = jnp.minimum(jnp.searchsorted(pad_end, jnp.minimum(blk_start, pad_end[-1] - 1), side="right"),
                             N_EXPERTS - 1).astype(jnp.int32)

    xs = _dispatch_rows(x, dest, pad_slots, p, tm)
    ys = _expert_ffn(xs, blk_expert, blk_valid, w1, w3, w2, tm_e, tf)
    return _moe_combine(x, ys, dest, gates, g, b, alpha, tm)


def _rope_tables(t_max):
    t = jnp.arange(t_max, dtype=jnp.int32)

    def freqs(pos, dim):
        inv = ROPE_THETA ** (-jnp.arange(0, dim, 2, dtype=F32) / dim)
        ang = pos.astype(F32)[:, None] * inv[None, :]
        return jnp.cos(ang), jnp.sin(ang)

    zeros = lambda w: jnp.zeros((t_max, w), F32)
    c, s = freqs(t, MLA_ROPE)
    half = MLA_ROPE // 2
    m_cos = jnp.concatenate([c, c, zeros(LANES - MLA_ROPE)], axis=1)
    m_sa = jnp.concatenate([-s, zeros(LANES - half)], axis=1)
    m_sb = jnp.concatenate([zeros(half), s, zeros(LANES - MLA_ROPE)], axis=1)
    cr, sr = freqs(t // GRID_W, HEAD_DIM // 2)
    cc, sc = freqs(t % GRID_W, HEAD_DIM // 2)
    q = HEAD_DIM // 4
    head_cos = jnp.concatenate([cr, cr, cc, cc], axis=1)
    head_sa = jnp.concatenate([-sr, zeros(q), -sc, zeros(q)], axis=1)
    head_sb = jnp.concatenate([zeros(q), sr, zeros(q), sc], axis=1)
    rep = lambda x: jnp.concatenate([x] * (LANES // HEAD_DIM), axis=1)
    return jnp.concatenate([m_cos, m_sa, m_sb, rep(head_cos), rep(head_sa), rep(head_sb)], axis=1)


def _chunk_heads(w, axis):
    shape = w.shape
    w = w.reshape(shape[:axis] + (HEADS, HEAD_DIM) + shape[axis + 1:])
    w = jnp.take(w, jnp.array(HEAD_PERM), axis=axis)
    return w.reshape(shape)


def _prep_layer(l, w_in, mla_q_norm, mla_kv_norm, w_uq, w_ukv, ax_q_norm, ax_k_norm,
                w_br_mla, w_br_win, w_br_ax, w_out, ln1_g, ln1_b, ln2_g, ln2_b):
    wi = w_in[l]
    sizes = [MLA_Q_RANK, MLA_KV_RANK, MLA_ROPE, QW, KW, KW, QW, KW, KW, N_BRANCH * D_MODEL]
    offs = [0]
    for s in sizes:
        offs.append(offs[-1] + s)
    cols = [wi[:, offs[j]:offs[j + 1]] for j in range(len(sizes))]
    c_q, c_kv, k_r, wq, wk, wv, aq, ak, av, wg = cols
    w_s = jnp.concatenate([c_q, c_kv, k_r, jnp.zeros((D_MODEL, LANES - MLA_ROPE), F32),
                           _chunk_heads(wq, 1), wk, wv, _chunk_heads(aq, 1), ak, av], axis=1).astype(BF16)
    uq = w_uq[l].reshape(MLA_Q_RANK, MLA_HEADS, MLA_QK)
    uq = jnp.pad(uq, ((0, 0), (0, 0), (0, MLA_QK_PAD - MLA_QK))).reshape(MLA_Q_RANK, MLA_HEADS * MLA_QK_PAD)
    ukv = w_ukv[l].reshape(MLA_KV_RANK, MLA_HEADS, MLA_NOPE + MLA_V)
    ukv = jnp.concatenate([ukv[:, :, :MLA_NOPE].reshape(MLA_KV_RANK, -1),
                           ukv[:, :, MLA_NOPE:].reshape(MLA_KV_RANK, -1)], axis=1)
    head_id = jnp.arange(LANES) // HEAD_DIM
    bd = jnp.where(head_id[:, None] == head_id[None, :], 1.0 / HEAD_DIM, 0.0).astype(BF16)
    row = lambda v: v.reshape(1, -1).astype(F32)
    return dict(
        w_s=w_s, w_g=wg.astype(BF16), w_uq=uq.astype(BF16), w_ukv=ukv.astype(BF16),
        g_q=row(mla_q_norm[l]), g_kv=row(mla_kv_norm[l]),
        g_aq=row(jnp.tile(ax_q_norm[l], LANES // HEAD_DIM)), g_ak=row(jnp.tile(ax_k_norm[l], LANES // HEAD_DIM)),
        bd=bd,
        w_br_mla=w_br_mla[l].astype(BF16),
        w_br_win=_chunk_heads(w_br_win[l], 0).astype(BF16),
        w_br_ax=_chunk_heads(w_br_ax[l], 0).astype(BF16),
        w_out=w_out[l].astype(BF16),
        ln1_g=row(ln1_g[l]), ln1_b=row(ln1_b[l]), ln2_g=row(ln2_g[l]), ln2_b=row(ln2_b[l]),
    )


def kernel(x_prompt, x_sample, emb_ln_g, emb_ln_b, w_in, mla_q_norm, mla_kv_norm, w_uq, w_ukv, win_sink,
           ax_q_norm, ax_k_norm, w_br_mla, w_br_win, w_br_ax, w_out, ln1_g, ln1_b, ln2_g, ln2_b,
           ffn_w1, ffn_w3, ffn_w2, moe_router, moe_w1, moe_w3, moe_w2):
    depth = w_in.shape[0]
    alpha = (2 * depth) ** 0.25
    groups = []
    off = 0
    for xg in (x_prompt, x_sample):
        bsz, seq, _ = xg.shape
        assert off % seq == 0 and seq % GRID_W == 0
        groups.append((off, bsz, seq))
        off += bsz * seq
    n = off
    seq_min = min(g[2] for g in groups)
    seq_max = max(g[2] for g in groups)
    tm = _pick_tile(seq_min, 512)
    tm_e = 512
    x = jnp.concatenate([x_prompt.reshape(-1, D_MODEL), x_sample.reshape(-1, D_MODEL)], axis=0)

    tab = _rope_tables(seq_max)
    pos_blk = jnp.concatenate([jnp.tile(jnp.arange(seq // tm, dtype=jnp.int32), bsz)
                               for (_, bsz, seq) in groups])

    x = _embed_ln(x, emb_ln_g, emb_ln_b, tm)
    o_a = o_b = o_c = None
    for l in range(depth):
        lw = _prep_layer(l, w_in, mla_q_norm, mla_kv_norm, w_uq, w_ukv, ax_q_norm, ax_k_norm,
                         w_br_mla, w_br_win, w_br_ax, w_out, ln1_g, ln1_b, ln2_g, ln2_b)
        qm, km, vm, wq, wk, wv, aq, ak, av = _in_proj(x, pos_blk, lw, tab, tm)
        sink = win_sink[l].astype(F32)
        for g in groups:
            o_a = _mla_attention(qm, km, vm, o_a, *g)
            o_b = _window_attention(wq, wk, wv, sink, o_b, *g)
            o_c = _axial_attention(aq, ak, av, o_c, *g)
        x = _merge(x, o_a, o_b, o_c, lw, alpha, tm)
        i = l // 2
        if l % 2 == 0:
            d_ff = ffn_w1.shape[2]
            x = _dense_ffn(x, ffn_w1[i].astype(BF16), ffn_w3[i].astype(BF16), ffn_w2[i].astype(BF16),
                           lw["ln2_g"], lw["ln2_b"], alpha, tm, d_ff)
        else:
            router = jnp.pad(moe_router[i], ((0, 0), (0, LANES - N_EXPERTS)))
            r_hi = router.astype(BF16)
            r_lo = (router - r_hi.astype(F32)).astype(BF16)
            d_ffe = moe_w1.shape[3]
            e_out, gates = _router(x, r_hi, r_lo, tm)
            x = _moe_layer(x, e_out, gates, moe_w1[i].astype(BF16), moe_w3[i].astype(BF16),
                           moe_w2[i].astype(BF16), lw["ln2_g"], lw["ln2_b"], alpha, tm, tm_e,
                           d_ffe // 2)
    n_p = x_prompt.shape[0] * x_prompt.shape[1]
    return (x[:n_p].reshape(x_prompt.shape), x[n_p:].reshape(x_sample.shape))
```

```python
import functools
import math

import jax
import jax.numpy as jnp
from jax import lax
from jax.experimental import pallas as pl
from jax.experimental.pallas import tpu as pltpu

F32 = jnp.float32
BF16 = jnp.bfloat16

D_MODEL = 1024
GRID_W = 64
ROPE_THETA = 10000.0
RMS_EPS = 1e-6
LN_EPS = 1e-5
NEG_INF = -1e30

MLA_HEADS = 4
MLA_Q_RANK = 384
MLA_KV_RANK = 256
MLA_NOPE = 128
MLA_ROPE = 64
MLA_V = 128
MLA_QK = MLA_NOPE + MLA_ROPE

HEADS = 8
KV_HEADS = 2
HEAD_DIM = 64
GROUP = HEADS // KV_HEADS
WINDOW = 128
Q_BLOCK = 128
SPAN = Q_BLOCK + 2 * WINDOW

N_BRANCH = 3
N_EXPERTS = 8
TOP_K = 2

LANES = 128
BF16_ROWS = 16
MXU_DIM = 256
VMEM_LIMIT = 56 * 1024 * 1024

MLA_QK_PAD = MXU_DIM
QW = HEADS * HEAD_DIM
KW = KV_HEADS * HEAD_DIM
C_CQ = 0
C_CKV = C_CQ + MLA_Q_RANK
C_KR = C_CKV + MLA_KV_RANK
C_WQ = C_KR + LANES
C_WK = C_WQ + QW
C_WV = C_WK + KW
C_AQ = C_WV + KW
C_AK = C_AQ + QW
C_AV = C_AK + KW
C_END = C_AV + KW
TAB_W = 6 * LANES
LOG2E = math.log2(math.e)
ATT_TK = 256

HEAD_PERM = [h for c in range(GROUP) for h in (c, GROUP + c)]


TOKEN_TILE = 512
EXPERT_TILE = 512


def _cparams(sem, vmem=VMEM_LIMIT):
    return pltpu.CompilerParams(dimension_semantics=sem, vmem_limit_bytes=vmem)


def _pick_tile(n, pref):
    t = min(pref, n)
    while n % t:
        t //= 2
    return t


def _ln_rows(y, g, b):
    mu = jnp.mean(y, axis=-1, keepdims=True)
    d = y - mu
    var = jnp.mean(d * d, axis=-1, keepdims=True)
    return d * lax.rsqrt(var + LN_EPS) * g + b


def _rms_rows(x, g):
    ms = jnp.mean(x * x, axis=-1, keepdims=True)
    return x * lax.rsqrt(ms + RMS_EPS) * g


def _rope_chunk(x, cos, sin_a, sin_b, half):
    return (x * cos + pltpu.roll(x, LANES - half, 1) * sin_a + pltpu.roll(x, half, 1) * sin_b)


def _dot(a, b):
    return jnp.dot(a, b, preferred_element_type=F32)


def _ln_kernel(x_ref, g_ref, b_ref, o_ref):
    o_ref[...] = _ln_rows(x_ref[...], g_ref[...], b_ref[...])


def _embed_ln(x, g, b, tm):
    n = x.shape[0]
    return pl.pallas_call(
        _ln_kernel,
        grid=(n // tm,),
        in_specs=[pl.BlockSpec((tm, D_MODEL), lambda i: (i, 0)),
                  pl.BlockSpec((1, D_MODEL), lambda i: (0, 0)),
                  pl.BlockSpec((1, D_MODEL), lambda i: (0, 0))],
        out_specs=pl.BlockSpec((tm, D_MODEL), lambda i: (i, 0)),
        out_shape=jax.ShapeDtypeStruct((n, D_MODEL), F32),
        compiler_params=_cparams(("parallel",)),
        name="embed_ln",
    )(x, g.reshape(1, -1), b.reshape(1, -1))


def _proj_kernel(pos_ref, x_ref, ws_ref, wuq_ref, wukv_ref, gq_ref, gkv_ref, gaq_ref, gak_ref,
                 bd_ref, tab_ref,
                 qm_ref, km_ref, vm_ref, wq_ref, wk_ref, wv_ref, aq_ref, ak_ref, av_ref):
    del pos_ref
    tm = x_ref.shape[0]
    x = x_ref[...].astype(BF16)
    res = _dot(x, ws_ref[...])

    m_cos = tab_ref[:, 0 * LANES:1 * LANES]
    m_sa = tab_ref[:, 1 * LANES:2 * LANES]
    m_sb = tab_ref[:, 2 * LANES:3 * LANES]
    a_cos = tab_ref[:, 3 * LANES:4 * LANES]
    a_sa = tab_ref[:, 4 * LANES:5 * LANES]
    a_sb = tab_ref[:, 5 * LANES:6 * LANES]
    c_q = _rms_rows(res[:, C_CQ:C_CKV], gq_ref[...]).astype(BF16)
    q = _dot(c_q, wuq_ref[...])
    q_scale = MLA_QK ** -0.5 * LOG2E
    for h in range(MLA_HEADS):
        base = h * MLA_QK_PAD
        qm_ref[:, base:base + LANES] = (q[:, base:base + LANES] * q_scale).astype(BF16)
        roped = _rope_chunk(q[:, base + LANES:base + 2 * LANES], m_cos, m_sa, m_sb, MLA_ROPE // 2)
        qm_ref[:, base + LANES:base + 2 * LANES] = (roped * q_scale).astype(BF16)

    c_kv = _rms_rows(res[:, C_CKV:C_KR], gkv_ref[...]).astype(BF16)
    kv = _dot(c_kv, wukv_ref[...])
    k_rope = _rope_chunk(res[:, C_KR:C_WQ], m_cos, m_sa, m_sb, MLA_ROPE // 2).astype(BF16)
    for h in range(MLA_HEADS):
        base = h * MLA_QK_PAD
        km_ref[:, base:base + LANES] = kv[:, h * LANES:(h + 1) * LANES].astype(BF16)
        km_ref[:, base + LANES:base + 2 * LANES] = k_rope
    v_off = MLA_HEADS * MLA_NOPE
    for r in range(tm // ATT_TK):
        rows = slice(r * ATT_TK, (r + 1) * ATT_TK)
        for h in range(MLA_HEADS):
            vm_ref[r, h * MLA_V:(h + 1) * MLA_V, :] = (
                kv[rows, v_off + h * MLA_V:v_off + (h + 1) * MLA_V].T.astype(BF16))
        av_ref[r] = res[rows, C_AV:C_END].T.astype(BF16)

    wq_ref[...] = (res[:, C_WQ:C_WK] * (HEAD_DIM ** -0.5 * LOG2E)).astype(BF16)
    wk_ref[...] = res[:, C_WK:C_WV].astype(BF16)
    for r in range(tm // Q_BLOCK):
        wv_ref[r] = res[r * Q_BLOCK:(r + 1) * Q_BLOCK, C_WV:C_AQ].T.astype(BF16)

    bd = bd_ref[...]

    def norm_rope(xc, g):
        ms = _dot((xc * xc).astype(BF16), bd)
        xn = xc * lax.rsqrt(ms + RMS_EPS) * g
        return _rope_chunk(xn, a_cos, a_sa, a_sb, HEAD_DIM // 4)

    for c in range(QW // LANES):
        xc = res[:, C_AQ + c * LANES:C_AQ + (c + 1) * LANES]
        aq_ref[:, c * LANES:(c + 1) * LANES] = (
            norm_rope(xc, gaq_ref[...]) * (HEAD_DIM ** -0.5 * LOG2E)).astype(BF16)
    ak_ref[...] = norm_rope(res[:, C_AK:C_AV], gak_ref[...]).astype(BF16)


def _in_proj(x, pos_blk, lw, tab, tm):
    n = x.shape[0]
    const = lambda shape: pl.BlockSpec(shape, lambda i, p: (0, 0))
    rows = lambda w: pl.BlockSpec((tm, w), lambda i, p: (i, 0))
    out_widths = [MLA_HEADS * MLA_QK_PAD, MLA_HEADS * MLA_QK_PAD, (MLA_HEADS * MLA_V, ATT_TK),
                  QW, KW, (KW, Q_BLOCK), QW, KW, (KW, ATT_TK)]

    def out_spec(w):
        if isinstance(w, tuple):
            return pl.BlockSpec((tm // w[1], w[0], w[1]), lambda i, p: (i, 0, 0))
        return rows(w)

    def out_shape(w):
        if isinstance(w, tuple):
            return jax.ShapeDtypeStruct((n // w[1], w[0], w[1]), BF16)
        return jax.ShapeDtypeStruct((n, w), BF16)

    grid_spec = pltpu.PrefetchScalarGridSpec(
        num_scalar_prefetch=1,
        grid=(n // tm,),
        in_specs=[rows(D_MODEL),
                  const((D_MODEL, C_END)),
                  const((MLA_Q_RANK, MLA_HEADS * MLA_QK_PAD)),
                  const((MLA_KV_RANK, 2 * MLA_HEADS * LANES)),
                  const((1, MLA_Q_RANK)), const((1, MLA_KV_RANK)),
                  const((1, LANES)), const((1, LANES)),
                  const((LANES, LANES)),
                  pl.BlockSpec((tm, TAB_W), lambda i, p: (p[i], 0))],
        out_specs=[out_spec(w) for w in out_widths],
    )
    return pl.pallas_call(
        _proj_kernel,
        grid_spec=grid_spec,
        out_shape=[out_shape(w) for w in out_widths],
        compiler_params=_cparams(("parallel",)),
        name="in_proj",
    )(pos_blk, x, lw["w_s"], lw["w_uq"], lw["w_ukv"], lw["g_q"], lw["g_kv"], lw["g_aq"], lw["g_ak"],
      lw["bd"], tab)


def _call_into(prev, kernel, operands, *, in_specs, **kwargs):
    if prev is None:
        prev = jnp.zeros(kwargs["out_shape"].shape, kwargs["out_shape"].dtype)
    n_in = len(operands)

    def skip_prev(*refs):
        return kernel(*refs[:n_in], *refs[n_in + 1:])

    return pl.pallas_call(skip_prev, in_specs=list(in_specs) + [pl.BlockSpec(memory_space=pl.ANY)],
                          input_output_aliases={n_in: 0}, **kwargs)(*operands, prev)


def _flash_t(qt_sc, k_ref, vt_ref, s0_sc, s1_sc, acc_sc, seq, v_rows):
    n_blk = acc_sc.shape[0]
    nk = seq // ATT_TK
    assert nk >= 2 and nk % 2 == 0
    acc_sc[...] = jnp.zeros(acc_sc.shape, F32)
    stat = lambda v: tuple(jnp.full((1, MXU_DIM), v, F32) for _ in range(n_blk))

    def scores(j, s_ref):
        k = k_ref[pl.ds(pl.multiple_of(j * ATT_TK, ATT_TK), ATT_TK), :]
        tops = []
        for b in range(n_blk):
            s = _dot(k, qt_sc[:, b * MXU_DIM:(b + 1) * MXU_DIM])
            s_ref[b] = s
            tops.append(jnp.max(s, axis=0, keepdims=True))
        return tuple(tops)

    ones_rows = jnp.where(lax.broadcasted_iota(jnp.int32, (BF16_ROWS, ATT_TK), 0) == 0, 1.0, 0.0).astype(BF16)

    def consume(j, s_ref, tops, ms):
        ms = list(ms)
        for b in range(n_blk):
            m_new = jnp.maximum(ms[b], tops[b])
            alpha = jnp.exp2(ms[b] - m_new)
            p = jnp.exp2((s_ref[b] - m_new).astype(BF16))
            ms[b] = m_new
            r0, r1 = v_rows[b]
            vt = jnp.concatenate([vt_ref[j, r0:r1, :], ones_rows], axis=0)
            acc_sc[b] = acc_sc[b] * alpha + _dot(vt, p)
        return tuple(ms)

    def pair(i, carry):
        ms, tops0 = carry
        tops1 = scores(2 * i + 1, s1_sc)
        ms = consume(2 * i, s0_sc, tops0, ms)
        tops0 = scores(2 * i + 2, s0_sc)
        return consume(2 * i + 1, s1_sc, tops1, ms), tops0

    trips = nk // 2 - 1
    unroll = trips if trips <= 5 else next(u for u in (5, 4, 3, 2, 1) if trips % u == 0)
    ms, tops0 = lax.fori_loop(0, trips, pair, (stat(NEG_INF), scores(0, s0_sc)), unroll=max(unroll, 1))
    tops1 = scores(nk - 1, s1_sc)
    ms = consume(nk - 2, s0_sc, tops0, ms)
    consume(nk - 1, s1_sc, tops1, ms)


def _stage_queries(q_ref, qt_sc):
    tq = q_ref.shape[0]
    lo = lax.broadcasted_iota(jnp.int32, (LANES, tq), 0) < HEAD_DIM
    for c in range(GROUP):
        qt = q_ref[:, c * LANES:(c + 1) * LANES].astype(F32).T
        qt_sc[:, c * tq:(c + 1) * tq] = jnp.where(lo, qt, 0.0).astype(BF16)
        qt_sc[:, (GROUP + c) * tq:(GROUP + c + 1) * tq] = jnp.where(lo, 0.0, qt).astype(BF16)


def _store_heads(head_out, o_ref):
    for c in range(GROUP):
        ot = jnp.concatenate([head_out(c), head_out(GROUP + c)], axis=0)
        o_ref[:, c * LANES:(c + 1) * LANES] = ot.T.astype(o_ref.dtype)


AXIAL_TQ = 128
AXIAL_SWEEPS = 4


def _gqa_kernel(q_ref, k_ref, vt_ref, o_ref, *scratch, seq):
    tq = AXIAL_TQ
    heads_per_blk = MXU_DIM // tq
    for t in range(AXIAL_SWEEPS):
        qt_sc, s0_sc, s1_sc, acc_sc = scratch[4 * t:4 * t + 4]
        rows = pl.ds(t * tq, tq)
        _stage_queries(q_ref.at[rows], qt_sc)
        v_rows = [((b * heads_per_blk) // GROUP * HEAD_DIM, ((b * heads_per_blk) // GROUP + 1) * HEAD_DIM)
                  for b in range(acc_sc.shape[0])]
        _flash_t(qt_sc, k_ref, vt_ref, s0_sc, s1_sc, acc_sc, seq, v_rows)

        def head_out(h, acc_sc=acc_sc):
            b, off = divmod(h * tq, MXU_DIM)
            return acc_sc[b, 0:HEAD_DIM, off:off + tq] / acc_sc[b, HEAD_DIM:HEAD_DIM + 1, off:off + tq]

        _store_heads(head_out, o_ref.at[rows])


def _axial_attention(q, k, vt, prev, tok_off, batch, seq):
    tq = AXIAL_TQ * AXIAL_SWEEPS
    nq = seq // tq
    nkb = seq // ATT_TK
    qb, sb = tok_off // tq, tok_off // seq
    n_blk = HEADS * AXIAL_TQ // MXU_DIM
    sweep_scratch = [pltpu.VMEM((KW, HEADS * AXIAL_TQ), BF16),
                     pltpu.VMEM((n_blk, ATT_TK, MXU_DIM), F32),
                     pltpu.VMEM((n_blk, ATT_TK, MXU_DIM), F32),
                     pltpu.VMEM((n_blk, HEAD_DIM + BF16_ROWS, MXU_DIM), F32)]
    return _call_into(
        prev, functools.partial(_gqa_kernel, seq=seq), (q, k, vt),
        grid=(batch, nq),
        in_specs=[pl.BlockSpec((tq, QW), lambda b, i: (qb + b * nq + i, 0)),
                  pl.BlockSpec((seq, KW), lambda b, i: (sb + b, 0)),
                  pl.BlockSpec((nkb, KW, ATT_TK), lambda b, i: (sb + b, 0, 0))],
        out_specs=pl.BlockSpec((tq, QW), lambda b, i: (qb + b * nq + i, 0)),
        out_shape=jax.ShapeDtypeStruct((q.shape[0], QW), BF16),
        scratch_shapes=sweep_scratch * AXIAL_SWEEPS,
        compiler_params=_cparams(("parallel", "arbitrary")),
        name="axial_attention",
    )


MLA_SWEEP_BLOCKS = 4
MLA_SWEEPS = 2


def _mla_kernel(q_ref, k_ref, vt_ref, o_ref, *scratch, seq):
    sweeps = len(scratch) // 4
    n_blk = q_ref.shape[0] // (sweeps * MXU_DIM)
    for t in range(sweeps):
        qt_sc, s0_sc, s1_sc, acc_sc = scratch[4 * t:4 * t + 4]
        for b in range(n_blk):
            rows = pl.ds((t * n_blk + b) * MXU_DIM, MXU_DIM)
            qt_sc[:, b * MXU_DIM:(b + 1) * MXU_DIM] = q_ref[rows, :].astype(F32).T.astype(BF16)
        _flash_t(qt_sc, k_ref, vt_ref, s0_sc, s1_sc, acc_sc, seq, [(0, MLA_V)] * n_blk)
        for b in range(n_blk):
            rows = pl.ds((t * n_blk + b) * MXU_DIM, MXU_DIM)
            o = acc_sc[b, 0:MLA_V, :] / acc_sc[b, MLA_V:MLA_V + 1, :]
            o_ref[rows, :] = o.T.astype(o_ref.dtype)


def _mla_attention(q, k, vt, prev, tok_off, batch, seq):
    tq = _pick_tile(seq, MLA_SWEEPS * MLA_SWEEP_BLOCKS * MXU_DIM)
    sweeps = MLA_SWEEPS if tq % (MLA_SWEEPS * MXU_DIM) == 0 else 1
    n_blk = tq // (sweeps * MXU_DIM)
    assert n_blk * sweeps * MXU_DIM == tq
    nq = seq // tq
    nkb = seq // ATT_TK
    qb, sb = tok_off // tq, tok_off // seq
    sweep_scratch = [pltpu.VMEM((MLA_QK_PAD, n_blk * MXU_DIM), BF16),
                     pltpu.VMEM((n_blk, ATT_TK, MXU_DIM), F32),
                     pltpu.VMEM((n_blk, ATT_TK, MXU_DIM), F32),
                     pltpu.VMEM((n_blk, MLA_V + BF16_ROWS, MXU_DIM), F32)]
    return _call_into(
        prev, functools.partial(_mla_kernel, seq=seq), (q, k, vt),
        grid=(batch, MLA_HEADS, nq),
        in_specs=[pl.BlockSpec((tq, MLA_QK_PAD), lambda b, h, i: (qb + b * nq + i, h)),
                  pl.BlockSpec((seq, MLA_QK_PAD), lambda b, h, i: (sb + b, h)),
                  pl.BlockSpec((nkb, MLA_V, ATT_TK), lambda b, h, i: (sb + b, h, 0))],
        out_specs=pl.BlockSpec((tq, MLA_V), lambda b, h, i: (qb + b * nq + i, h)),
        out_shape=jax.ShapeDtypeStruct((q.shape[0], MLA_HEADS * MLA_V), BF16),
        scratch_shapes=sweep_scratch * sweeps,
        compiler_params=_cparams(("parallel", "parallel", "arbitrary")),
        name="mla_attention",
    )


WINDOW_SWEEPS = 8


def _window_kernel(sink_ref, q_ref, k_ref, vt_ref, o_ref, *qt_scs, seq):
    for t, qt_sc in enumerate(qt_scs):
        rows = pl.ds(t * Q_BLOCK, Q_BLOCK)
        _window_block(sink_ref, q_ref.at[rows], k_ref, vt_ref, o_ref.at[rows], qt_sc,
                      (pl.program_id(1) * len(qt_scs) + t) * Q_BLOCK, seq)


def _window_block(sink_ref, q_ref, k_ref, vt_ref, o_ref, qt_sc, start, seq):
    kstart = pl.multiple_of(jnp.clip(start - WINDOW, 0, seq - SPAN), Q_BLOCK)
    kb = kstart // Q_BLOCK
    _stage_queries(q_ref, qt_sc)
    s_all = _dot(k_ref[pl.ds(kstart, SPAN), :], qt_sc[...])
    vt = jnp.concatenate([vt_ref[kb + j] for j in range(SPAN // Q_BLOCK)], axis=1)
    ones_rows = jnp.where(lax.broadcasted_iota(jnp.int32, (BF16_ROWS, SPAN), 0) == 0, 1.0, 0.0).astype(BF16)
    rel = ((kstart + lax.broadcasted_iota(jnp.int32, (SPAN, Q_BLOCK), 0))
           - (start + lax.broadcasted_iota(jnp.int32, (SPAN, Q_BLOCK), 1)))
    dist_i = jnp.abs(rel)
    dist = jnp.where(dist_i <= WINDOW, dist_i.astype(F32), -NEG_INF * 2.0 ** HEADS)
    outs = []
    for blk in range(HEADS // 2):
        ps, sinks = [], []
        for h in (2 * blk, 2 * blk + 1):
            slope = 2.0 ** (-8.0 * (h + 1) / HEADS) * LOG2E
            sink = sink_ref[h] * LOG2E
            logits = s_all[:, h * Q_BLOCK:(h + 1) * Q_BLOCK] - slope * dist
            m = jnp.maximum(jnp.max(logits, axis=0, keepdims=True), sink)
            ps.append(jnp.exp2((logits - m).astype(BF16)))
            sinks.append(jnp.exp2(sink - m))
        g = (2 * blk) // GROUP
        lhs = jnp.concatenate([vt[g * HEAD_DIM:(g + 1) * HEAD_DIM], ones_rows], axis=0)
        pv = _dot(lhs, jnp.concatenate(ps, axis=1))
        denom = pv[HEAD_DIM:HEAD_DIM + 1] + jnp.concatenate(sinks, axis=1)
        outs.append(pv[0:HEAD_DIM] / denom)
    _store_heads(lambda h: outs[h // 2][:, (h % 2) * Q_BLOCK:(h % 2 + 1) * Q_BLOCK], o_ref)


def _window_attention(q, k, v, sink, prev, tok_off, batch, seq):
    tq = _pick_tile(seq, WINDOW_SWEEPS * Q_BLOCK)
    assert seq >= SPAN and tq % Q_BLOCK == 0
    nq = seq // tq
    qb, sb = tok_off // tq, tok_off // seq
    return _call_into(
        prev, functools.partial(_window_kernel, seq=seq), (sink, q, k, v),
        grid=(batch, nq),
        in_specs=[pl.BlockSpec(memory_space=pltpu.SMEM),
                  pl.BlockSpec((tq, QW), lambda b, i: (qb + b * nq + i, 0)),
                  pl.BlockSpec((seq, KW), lambda b, i: (sb + b, 0)),
                  pl.BlockSpec((seq // Q_BLOCK, KW, Q_BLOCK), lambda b, i: (sb + b, 0, 0))],
        out_specs=pl.BlockSpec((tq, QW), lambda b, i: (qb + b * nq + i, 0)),
        out_shape=jax.ShapeDtypeStruct((q.shape[0], QW), BF16),
        scratch_shapes=[pltpu.VMEM((KW, HEADS * Q_BLOCK), BF16)] * (tq // Q_BLOCK),
        compiler_params=_cparams(("parallel", "arbitrary")),
        name="window_attention",
    )


def _route(x, w_hi, w_lo):
    hi = x.astype(BF16)
    lo = (x - hi.astype(F32)).astype(BF16)
    logits = _dot(hi, w_hi) + _dot(lo, w_hi) + _dot(hi, w_lo)
    lane = lax.broadcasted_iota(jnp.int32, logits.shape, 1)
    logits = jnp.where(lane < N_EXPERTS, logits, NEG_INF)
    t1 = jnp.max(logits, axis=1, keepdims=True)
    e1 = jnp.min(jnp.where(logits == t1, lane, LANES), axis=1, keepdims=True)
    rest = jnp.where(lane == e1, NEG_INF, logits)
    t2 = jnp.max(rest, axis=1, keepdims=True)
    e2 = jnp.min(jnp.where(rest == t2, lane, LANES), axis=1, keepdims=True)
    w = jnp.exp(t2 - t1)
    g1 = 1.0 / (1.0 + w)
    g2 = w / (1.0 + w)
    return (jnp.where(lane == 0, e1, jnp.where(lane == 1, e2, 0)),
            jnp.where(lane == 0, g1, jnp.where(lane == 1, g2, 0.0)))


def _router_kernel(x_ref, whi_ref, wlo_ref, e_ref, gate_ref):
    e_ref[...], gate_ref[...] = _route(x_ref[...], whi_ref[...], wlo_ref[...])


def _router(x, w_hi, w_lo, tm):
    n = x.shape[0]
    return pl.pallas_call(
        _router_kernel,
        grid=(n // tm,),
        in_specs=[pl.BlockSpec((tm, D_MODEL), lambda i: (i, 0)),
                  pl.BlockSpec((D_MODEL, LANES), lambda i: (0, 0)),
                  pl.BlockSpec((D_MODEL, LANES), lambda i: (0, 0))],
        out_specs=[pl.BlockSpec((tm, LANES), lambda i: (i, 0)),
                   pl.BlockSpec((tm, LANES), lambda i: (i, 0))],
        out_shape=[jax.ShapeDtypeStruct((n, LANES), jnp.int32),
                   jax.ShapeDtypeStruct((n, LANES), F32)],
        compiler_params=_cparams(("parallel",)),
        name="moe_router",
    )(x, w_hi, w_lo)


def _merge_kernel(x_ref, oa_ref, ob_ref, oc_ref, wg_ref, wa_ref, wb_ref, wc_ref, wo_ref, g_ref, b_ref,
                  o_ref, *, alpha):
    x = x_ref[...]
    xb = x.astype(BF16)
    merged = None
    for idx, (o_br, w_br) in enumerate(((oa_ref, wa_ref), (ob_ref, wb_ref), (oc_ref, wc_ref))):
        gate = jax.nn.sigmoid(_dot(xb, wg_ref[:, idx * D_MODEL:(idx + 1) * D_MODEL]))
        term = gate * _dot(o_br[...], w_br[...])
        merged = term if merged is None else merged + term
    m = _dot(merged.astype(BF16), wo_ref[...])
    o_ref[...] = _ln_rows(alpha * x + m, g_ref[...], b_ref[...])


def _merge(x, o_a, o_b, o_c, lw, alpha, tm):
    n = x.shape[0]
    const = lambda shape: pl.BlockSpec(shape, lambda i: (0, 0))
    rows = lambda w: pl.BlockSpec((tm, w), lambda i: (i, 0))
    return pl.pallas_call(
        functools.partial(_merge_kernel, alpha=alpha),
        grid=(n // tm,),
        in_specs=[rows(D_MODEL), rows(MLA_HEADS * MLA_V), rows(QW), rows(QW),
                  const((D_MODEL, N_BRANCH * D_MODEL)),
                  const((MLA_HEADS * MLA_V, D_MODEL)), const((QW, D_MODEL)), const((QW, D_MODEL)),
                  const((D_MODEL, D_MODEL)), const((1, D_MODEL)), const((1, D_MODEL))],
        out_specs=rows(D_MODEL),
        out_shape=jax.ShapeDtypeStruct((n, D_MODEL), F32),
        compiler_params=_cparams(("parallel",)),
        name="gated_merge",
    )(x, o_a, o_b, o_c, lw["w_g"], lw["w_br_mla"], lw["w_br_win"], lw["w_br_ax"], lw["w_out"],
      lw["ln1_g"], lw["ln1_b"])


def _ffn_kernel(x_ref, w1_ref, w3_ref, w2_ref, g_ref, b_ref, o_ref, acc_ref, *, alpha):
    f = pl.program_id(1)

    @pl.when(f == 0)
    def _():
        acc_ref[...] = jnp.zeros(acc_ref.shape, F32)

    xb = x_ref[...].astype(BF16)
    h = jax.nn.silu(_dot(xb, w1_ref[...])) * _dot(xb, w3_ref[...])
    acc_ref[...] += _dot(h.astype(BF16), w2_ref[...])

    @pl.when(f == pl.num_programs(1) - 1)
    def _():
        o_ref[...] = _ln_rows(alpha * x_ref[...] + acc_ref[...], g_ref[...], b_ref[...])


def _dense_ffn(x, w1, w3, w2, g, b, alpha, tm, tf):
    n = x.shape[0]
    d_ff = w1.shape[1]
    mode = dict(pipeline_mode=pl.Buffered(1)) if tf == d_ff else {}
    return pl.pallas_call(
        functools.partial(_ffn_kernel, alpha=alpha),
        grid=(n // tm, d_ff // tf),
        in_specs=[pl.BlockSpec((tm, D_MODEL), lambda i, f: (i, 0)),
                  pl.BlockSpec((D_MODEL, tf), lambda i, f: (0, f), **mode),
                  pl.BlockSpec((D_MODEL, tf), lambda i, f: (0, f), **mode),
                  pl.BlockSpec((tf, D_MODEL), lambda i, f: (f, 0), **mode),
                  pl.BlockSpec((1, D_MODEL), lambda i, f: (0, 0)),
                  pl.BlockSpec((1, D_MODEL), lambda i, f: (0, 0))],
        out_specs=pl.BlockSpec((tm, D_MODEL), lambda i, f: (i, 0)),
        out_shape=jax.ShapeDtypeStruct((n, D_MODEL), F32),
        scratch_shapes=[pltpu.VMEM((tm, D_MODEL), F32)],
        compiler_params=_cparams(("parallel", "arbitrary")),
        name="dense_ffn",
    )(x, w1, w3, w2, g, b)


ROW_TILE = 8


def _row_of(ref3, r):
    return ref3.at[lax.shift_right_logical(r, 3), pl.ds(jnp.bitwise_and(r, ROW_TILE - 1), 1)]


def _dispatch_kernel(pad_ref, dest_hbm, x_ref, xs_hbm, dest_smem, zero_ref, idx_sem, row_sem, *, n_pad, zpad):
    i = pl.program_id(0)
    groups = x_ref.shape[0]
    n_idx = TOP_K * ROW_TILE * groups
    cp = pltpu.make_async_copy(dest_hbm.at[pl.ds(pl.multiple_of(i * n_idx, n_idx), n_idx)], dest_smem, idx_sem)
    cp.start()
    zero_ref[...] = jnp.zeros(zero_ref.shape, F32)
    cp.wait()

    def send(g, carry):
        for u in range(ROW_TILE):
            for k in range(TOP_K):
                d = dest_smem[(g * ROW_TILE + u) * TOP_K + k]
                pltpu.make_async_copy(x_ref.at[g, pl.ds(u, 1)], _row_of(xs_hbm, d), row_sem).start(priority=k)
        return carry

    lax.fori_loop(0, groups, send, 0)

    def pad_copy(r):
        return pltpu.make_async_copy(zero_ref.at[pl.ds(0, 1)], _row_of(xs_hbm, pad_ref[i * zpad + r]), row_sem)

    def send_zero(r, carry):
        @pl.when(i * zpad + r < n_pad)
        def _():
            pad_copy(r).start()
        return carry

    def drain_zero(r, carry):
        @pl.when(i * zpad + r < n_pad)
        def _():
            pad_copy(r).wait()
        return carry

    def drain(r, carry):
        pltpu.make_async_copy(x_ref.at[0, pl.ds(0, 1)], xs_hbm.at[0, pl.ds(0, 1)], row_sem).wait()
        return carry

    lax.fori_loop(0, zpad, send_zero, 0)
    lax.fori_loop(0, n_idx, drain, 0, unroll=8)
    lax.fori_loop(0, zpad, drain_zero, 0)


def _dispatch_rows(x, dest, pad_slots, p, tm):
    n = x.shape[0]
    steps = n // tm
    n_pad = pad_slots.shape[0]
    zpad = -(-n_pad // steps)
    grid_spec = pltpu.PrefetchScalarGridSpec(
        num_scalar_prefetch=1,
        grid=(steps,),
        in_specs=[pl.BlockSpec(memory_space=pl.ANY),
                  pl.BlockSpec((tm // ROW_TILE, ROW_TILE, D_MODEL), lambda i, pad: (i, 0, 0))],
        out_specs=pl.BlockSpec(memory_space=pl.ANY),
        scratch_shapes=[pltpu.SMEM((TOP_K * tm,), jnp.int32),
                        pltpu.VMEM((ROW_TILE, D_MODEL), F32),
                        pltpu.SemaphoreType.DMA, pltpu.SemaphoreType.DMA],
    )
    xs = pl.pallas_call(
        functools.partial(_dispatch_kernel, n_pad=n_pad, zpad=zpad),
        grid_spec=grid_spec,
        out_shape=jax.ShapeDtypeStruct((p // ROW_TILE, ROW_TILE, D_MODEL), F32),
        compiler_params=_cparams(("arbitrary",)),
        name="moe_dispatch",
    )(pad_slots, dest, x.reshape(n // ROW_TILE, ROW_TILE, D_MODEL))
    return xs.reshape(p, D_MODEL)


def _expert_kernel(be_ref, bv_ref, x_ref, w1_ref, w3_ref, w2_ref, o_ref, acc_ref):
    del be_ref
    i = pl.program_id(0)
    f = pl.program_id(1)
    valid = bv_ref[i] != 0

    @pl.when(f == 0)
    def _():
        acc_ref[...] = jnp.zeros(acc_ref.shape, F32)

    @pl.when(valid)
    def _():
        xb = x_ref[...].astype(BF16)
        h = jax.nn.silu(_dot(xb, w1_ref[0])) * _dot(xb, w3_ref[0])
        acc_ref[...] += _dot(h.astype(BF16), w2_ref[0])

    @pl.when(f == pl.num_programs(1) - 1)
    def _():
        o_ref[...] = acc_ref[...]


def _expert_ffn(xs, blk_expert, blk_valid, w1, w3, w2, tm, tf):
    p = xs.shape[0]
    d_ff = w1.shape[2]
    nf = d_ff // tf
    fidx = lambda i, f, be, bv: jnp.where(bv[i] != 0, f, nf - 1)
    grid_spec = pltpu.PrefetchScalarGridSpec(
        num_scalar_prefetch=2,
        grid=(p // tm, nf),
        in_specs=[pl.BlockSpec((tm, D_MODEL), lambda i, f, be, bv: (i, 0)),
                  pl.BlockSpec((1, D_MODEL, tf), lambda i, f, be, bv: (be[i], 0, fidx(i, f, be, bv))),
                  pl.BlockSpec((1, D_MODEL, tf), lambda i, f, be, bv: (be[i], 0, fidx(i, f, be, bv))),
                  pl.BlockSpec((1, tf, D_MODEL), lambda i, f, be, bv: (be[i], fidx(i, f, be, bv), 0))],
        out_specs=pl.BlockSpec((tm, D_MODEL), lambda i, f, be, bv: (i, 0)),
        scratch_shapes=[pltpu.VMEM((tm, D_MODEL), F32)],
    )
    return pl.pallas_call(
        _expert_kernel,
        grid_spec=grid_spec,
        out_shape=jax.ShapeDtypeStruct((p, D_MODEL), F32),
        compiler_params=_cparams(("arbitrary", "arbitrary")),
        name="moe_experts",
    )(blk_expert, blk_valid, xs, w1, w3, w2)


def _combine_kernel(pos_hbm, ys_hbm, x_ref, gate_ref, g_ref, b_ref, o_ref,
                    r0_ref, r1_ref, pos_smem, pos_sem, row_sem, *, alpha):
    i = pl.program_id(0)
    tm = x_ref.shape[0]
    n_idx = TOP_K * tm
    cp = pltpu.make_async_copy(pos_hbm.at[pl.ds(pl.multiple_of(i * n_idx, n_idx), n_idx)], pos_smem, pos_sem)
    cp.start()
    cp.wait()

    def fetch(g, carry):
        for u in range(ROW_TILE):
            for k, buf in enumerate((r0_ref, r1_ref)):
                src = _row_of(ys_hbm, pos_smem[(g * ROW_TILE + u) * TOP_K + k])
                pltpu.make_async_copy(src, buf.at[g, pl.ds(u, 1)], row_sem).start(priority=k)
        return carry

    def drain(r, carry):
        pltpu.make_async_copy(ys_hbm.at[0, pl.ds(0, 1)], r0_ref.at[0, pl.ds(0, 1)], row_sem).wait()
        return carry

    lax.fori_loop(0, tm // ROW_TILE, fetch, 0)
    lax.fori_loop(0, n_idx, drain, 0, unroll=8)
    gates = gate_ref[...]
    y = (gates[:, 0:1] * r0_ref[...].reshape(tm, D_MODEL) + gates[:, 1:2] * r1_ref[...].reshape(tm, D_MODEL))
    o_ref[...] = _ln_rows(alpha * x_ref[...] + y, g_ref[...], b_ref[...])


def _moe_combine(x, ys, pos, gates, g, b, alpha, tm):
    n = x.shape[0]
    return pl.pallas_call(
        functools.partial(_combine_kernel, alpha=alpha),
        grid=(n // tm,),
        in_specs=[pl.BlockSpec(memory_space=pl.ANY), pl.BlockSpec(memory_space=pl.ANY),
                  pl.BlockSpec((tm, D_MODEL), lambda i: (i, 0)),
                  pl.BlockSpec((tm, LANES), lambda i: (i, 0)),
                  pl.BlockSpec((1, D_MODEL), lambda i: (0, 0)),
                  pl.BlockSpec((1, D_MODEL), lambda i: (0, 0))],
        out_specs=pl.BlockSpec((tm, D_MODEL), lambda i: (i, 0)),
        out_shape=jax.ShapeDtypeStruct((n, D_MODEL), F32),
        scratch_shapes=[pltpu.VMEM((tm // ROW_TILE, ROW_TILE, D_MODEL), F32),
                        pltpu.VMEM((tm // ROW_TILE, ROW_TILE, D_MODEL), F32),
                        pltpu.SMEM((TOP_K * tm,), jnp.int32),
                        pltpu.SemaphoreType.DMA, pltpu.SemaphoreType.DMA],
        compiler_params=_cparams(("arbitrary",)),
        name="moe_combine",
    )(pos, ys.reshape(ys.shape[0] // ROW_TILE, ROW_TILE, D_MODEL), x, gates, g, b)


def _moe_layer(x, e_out, gates, w1, w3, w2, g, b, alpha, tm, tm_e, tf):
    n = x.shape[0]
    a = n * TOP_K
    e_flat = e_out[:, :TOP_K].reshape(a)
    onehot = (e_flat[:, None] == jnp.arange(N_EXPERTS, dtype=jnp.int32)[None, :]).astype(jnp.int32)
    csum = jnp.cumsum(onehot, axis=0)
    counts = csum[-1]
    rank = jnp.sum((csum - 1) * onehot, axis=1)
    padded = (counts + tm_e - 1) // tm_e * tm_e
    pad_end = jnp.cumsum(padded)
    pad_start = pad_end - padded
    dest = (pad_start[e_flat] + rank).astype(jnp.int32)
    assert a % tm_e == 0
    n_pad = N_EXPERTS * tm_e
    p = a + n_pad
    npad = padded - counts
    pad_cum = jnp.cumsum(npad)
    pad_i = jnp.arange(n_pad, dtype=jnp.int32)
    pad_e = jnp.searchsorted(pad_cum, pad_i, side="right")
    pad_ec = jnp.minimum(pad_e, N_EXPERTS - 1)
    pad_slots = jnp.where(pad_e >= N_EXPERTS, pad_end[-1] + (pad_i - pad_cum[-1]),
                          (pad_start + counts)[pad_ec] + (pad_i - (pad_cum - npad)[pad_ec])).astype(jnp.int32)
    blk_start = jnp.arange(p // tm_e, dtype=jnp.int32) * tm_e
    blk_valid = (blk_start < pad_end[-1]).astype(jnp.int32)
    blk_expert = jnp.minimum(jnp.searchsorted(pad_end, jnp.minimum(blk_start, pad_end[-1] - 1), side="right"),
                             N_EXPERTS - 1).astype(jnp.int32)

    xs = _dispatch_rows(x, dest, pad_slots, p, tm)
    ys = _expert_ffn(xs, blk_expert, blk_valid, w1, w3, w2, tm_e, tf)
    return _moe_combine(x, ys, dest, gates, g, b, alpha, tm)


def _rope_tables(t_max):
    t = jnp.arange(t_max, dtype=jnp.int32)

    def freqs(pos, dim):
        inv = ROPE_THETA ** (-jnp.arange(0, dim, 2, dtype=F32) / dim)
        ang = pos.astype(F32)[:, None] * inv[None, :]
        return jnp.cos(ang), jnp.sin(ang)

    zeros = lambda w: jnp.zeros((t_max, w), F32)
    c, s = freqs(t, MLA_ROPE)
    half = MLA_ROPE // 2
    m_cos = jnp.concatenate([c, c, zeros(LANES - MLA_ROPE)], axis=1)
    m_sa = jnp.concatenate([-s, zeros(LANES - half)], axis=1)
    m_sb = jnp.concatenate([zeros(half), s, zeros(LANES - MLA_ROPE)], axis=1)
    cr, sr = freqs(t // GRID_W, HEAD_DIM // 2)
    cc, sc = freqs(t % GRID_W, HEAD_DIM // 2)
    q = HEAD_DIM // 4
    head_cos = jnp.concatenate([cr, cr, cc, cc], axis=1)
    head_sa = jnp.concatenate([-sr, zeros(q), -sc, zeros(q)], axis=1)
    head_sb = jnp.concatenate([zeros(q), sr, zeros(q), sc], axis=1)
    rep = lambda x: jnp.concatenate([x] * (LANES // HEAD_DIM), axis=1)
    return jnp.concatenate([m_cos, m_sa, m_sb, rep(head_cos), rep(head_sa), rep(head_sb)], axis=1)


def _chunk_heads(w, axis):
    shape = w.shape
    w = w.reshape(shape[:axis] + (HEADS, HEAD_DIM) + shape[axis + 1:])
    w = jnp.take(w, jnp.array(HEAD_PERM), axis=axis)
    return w.reshape(shape)


def _prep_layer(l, w_in, mla_q_norm, mla_kv_norm, w_uq, w_ukv, ax_q_norm, ax_k_norm,
                w_br_mla, w_br_win, w_br_ax, w_out, ln1_g, ln1_b, ln2_g, ln2_b):
    wi = w_in[l]
    sizes = [MLA_Q_RANK, MLA_KV_RANK, MLA_ROPE, QW, KW, KW, QW, KW, KW, N_BRANCH * D_MODEL]
    offs = [0]
    for s in sizes:
        offs.append(offs[-1] + s)
    cols = [wi[:, offs[j]:offs[j + 1]] for j in range(len(sizes))]
    c_q, c_kv, k_r, wq, wk, wv, aq, ak, av, wg = cols
    w_s = jnp.concatenate([c_q, c_kv, k_r, jnp.zeros((D_MODEL, LANES - MLA_ROPE), F32),
                           _chunk_heads(wq, 1), wk, wv, _chunk_heads(aq, 1), ak, av], axis=1).astype(BF16)
    uq = w_uq[l].reshape(MLA_Q_RANK, MLA_HEADS, MLA_QK)
    uq = jnp.pad(uq, ((0, 0), (0, 0), (0, MLA_QK_PAD - MLA_QK))).reshape(MLA_Q_RANK, MLA_HEADS * MLA_QK_PAD)
    ukv = w_ukv[l].reshape(MLA_KV_RANK, MLA_HEADS, MLA_NOPE + MLA_V)
    ukv = jnp.concatenate([ukv[:, :, :MLA_NOPE].reshape(MLA_KV_RANK, -1),
                           ukv[:, :, MLA_NOPE:].reshape(MLA_KV_RANK, -1)], axis=1)
    head_id = jnp.arange(LANES) // HEAD_DIM
    bd = jnp.where(head_id[:, None] == head_id[None, :], 1.0 / HEAD_DIM, 0.0).astype(BF16)
    row = lambda v: v.reshape(1, -1).astype(F32)
    return dict(
        w_s=w_s, w_g=wg.astype(BF16), w_uq=uq.astype(BF16), w_ukv=ukv.astype(BF16),
        g_q=row(mla_q_norm[l]), g_kv=row(mla_kv_norm[l]),
        g_aq=row(jnp.tile(ax_q_norm[l], LANES // HEAD_DIM)), g_ak=row(jnp.tile(ax_k_norm[l], LANES // HEAD_DIM)),
        bd=bd,
        w_br_mla=w_br_mla[l].astype(BF16),
        w_br_win=_chunk_heads(w_br_win[l], 0).astype(BF16),
        w_br_ax=_chunk_heads(w_br_ax[l], 0).astype(BF16),
        w_out=w_out[l].astype(BF16),
        ln1_g=row(ln1_g[l]), ln1_b=row(ln1_b[l]), ln2_g=row(ln2_g[l]), ln2_b=row(ln2_b[l]),
    )


def kernel(x_prompt, x_sample, emb_ln_g, emb_ln_b, w_in, mla_q_norm, mla_kv_norm, w_uq, w_ukv, win_sink,
           ax_q_norm, ax_k_norm, w_br_mla, w_br_win, w_br_ax, w_out, ln1_g, ln1_b, ln2_g, ln2_b,
           ffn_w1, ffn_w3, ffn_w2, moe_router, moe_w1, moe_w3, moe_w2):
    depth = w_in.shape[0]
    alpha = (2 * depth) ** 0.25
    groups = []
    off = 0
    for xg in (x_prompt, x_sample):
        bsz, seq, _ = xg.shape
        assert off % seq == 0 and seq % GRID_W == 0
        groups.append((off, bsz, seq))
        off += bsz * seq
    n = off
    seq_min = min(g[2] for g in groups)
    seq_max = max(g[2] for g in groups)
    tm = _pick_tile(seq_min, TOKEN_TILE)
    tm_e = EXPERT_TILE
    x = jnp.concatenate([x_prompt.reshape(-1, D_MODEL), x_sample.reshape(-1, D_MODEL)], axis=0)

    tab = _rope_tables(seq_max)
    pos_blk = jnp.concatenate([jnp.tile(jnp.arange(seq // tm, dtype=jnp.int32), bsz)
                               for (_, bsz, seq) in groups])

    x = _embed_ln(x, emb_ln_g, emb_ln_b, tm)
    o_a = o_b = o_c = None
    for l in range(depth):
        lw = _prep_layer(l, w_in, mla_q_norm, mla_kv_norm, w_uq, w_ukv, ax_q_norm, ax_k_norm,
                         w_br_mla, w_br_win, w_br_ax, w_out, ln1_g, ln1_b, ln2_g, ln2_b)
        qm, km, vm, wq, wk, wv, aq, ak, av = _in_proj(x, pos_blk, lw, tab, tm)
        sink = win_sink[l].astype(F32)
        for g in groups:
            o_a = _mla_attention(qm, km, vm, o_a, *g)
            o_b = _window_attention(wq, wk, wv, sink, o_b, *g)
            o_c = _axial_attention(aq, ak, av, o_c, *g)
        x = _merge(x, o_a, o_b, o_c, lw, alpha, tm)
        i = l // 2
        if l % 2 == 0:
            d_ff = ffn_w1.shape[2]
            x = _dense_ffn(x, ffn_w1[i].astype(BF16), ffn_w3[i].astype(BF16), ffn_w2[i].astype(BF16),
                           lw["ln2_g"], lw["ln2_b"], alpha, tm, d_ff)
        else:
            router = jnp.pad(moe_router[i], ((0, 0), (0, LANES - N_EXPERTS)))
            r_hi = router.astype(BF16)
            r_lo = (router - r_hi.astype(F32)).astype(BF16)
            d_ffe = moe_w1.shape[3]
            e_out, gates = _router(x, r_hi, r_lo, tm)
            x = _moe_layer(x, e_out, gates, moe_w1[i].astype(BF16), moe_w3[i].astype(BF16),
                           moe_w2[i].astype(BF16), lw["ln2_g"], lw["ln2_b"], alpha, tm, tm_e,
                           d_ffe // 2)
    n_p = x_prompt.shape[0] * x_prompt.shape[1]
    return (x[:n_p].reshape(x_prompt.shape), x[n_p:].reshape(x_sample.shape))
```

```python
import functools
import math

import jax
import jax.numpy as jnp
from jax import lax
from jax.experimental import pallas as pl
from jax.experimental.pallas import tpu as pltpu

F32 = jnp.float32
BF16 = jnp.bfloat16

D_MODEL = 1024
GRID_W = 64
ROPE_THETA = 10000.0
RMS_EPS = 1e-6
LN_EPS = 1e-5
NEG_INF = -1e30

MLA_HEADS = 4
MLA_Q_RANK = 384
MLA_KV_RANK = 256
MLA_NOPE = 128
MLA_ROPE = 64
MLA_V = 128
MLA_QK = MLA_NOPE + MLA_ROPE

HEADS = 8
KV_HEADS = 2
HEAD_DIM = 64
GROUP = HEADS // KV_HEADS
WINDOW = 128
Q_BLOCK = 128
SPAN = Q_BLOCK + 2 * WINDOW

N_BRANCH = 3
N_EXPERTS = 8
TOP_K = 2

LANES = 128
BF16_ROWS = 16
MXU_DIM = 256
VMEM_LIMIT = 56 * 1024 * 1024

MLA_QK_PAD = MXU_DIM
QW = HEADS * HEAD_DIM
KW = KV_HEADS * HEAD_DIM
C_CQ = 0
C_CKV = C_CQ + MLA_Q_RANK
C_KR = C_CKV + MLA_KV_RANK
C_WQ = C_KR + LANES
C_WK = C_WQ + QW
C_WV = C_WK + KW
C_AQ = C_WV + KW
C_AK = C_AQ + QW
C_AV = C_AK + KW
C_END = C_AV + KW
TAB_W = 6 * LANES
LOG2E = math.log2(math.e)
ATT_TK = 256

HEAD_PERM = [h for c in range(GROUP) for h in (c, GROUP + c)]


TOKEN_TILE = 512
EXPERT_TILE = 512


def _cparams(sem, vmem=VMEM_LIMIT):
    return pltpu.CompilerParams(dimension_semantics=sem, vmem_limit_bytes=vmem)


def _pick_tile(n, pref):
    t = min(pref, n)
    while n % t:
        t //= 2
    return t


def _ln_rows(y, g, b):
    mu = jnp.mean(y, axis=-1, keepdims=True)
    d = y - mu
    var = jnp.mean(d * d, axis=-1, keepdims=True)
    return d * lax.rsqrt(var + LN_EPS) * g + b


def _rms_rows(x, g):
    ms = jnp.mean(x * x, axis=-1, keepdims=True)
    return x * lax.rsqrt(ms + RMS_EPS) * g


def _rope_chunk(x, cos, sin_a, sin_b, half):
    return (x * cos + pltpu.roll(x, LANES - half, 1) * sin_a + pltpu.roll(x, half, 1) * sin_b)


def _dot(a, b):
    return jnp.dot(a, b, preferred_element_type=F32)


def _ln_kernel(x_ref, g_ref, b_ref, o_ref):
    o_ref[...] = _ln_rows(x_ref[...], g_ref[...], b_ref[...])


def _embed_ln(x, g, b, tm):
    n = x.shape[0]
    return pl.pallas_call(
        _ln_kernel,
        grid=(n // tm,),
        in_specs=[pl.BlockSpec((tm, D_MODEL), lambda i: (i, 0)),
                  pl.BlockSpec((1, D_MODEL), lambda i: (0, 0)),
                  pl.BlockSpec((1, D_MODEL), lambda i: (0, 0))],
        out_specs=pl.BlockSpec((tm, D_MODEL), lambda i: (i, 0)),
        out_shape=jax.ShapeDtypeStruct((n, D_MODEL), F32),
        compiler_params=_cparams(("parallel",)),
        name="embed_ln",
    )(x, g.reshape(1, -1), b.reshape(1, -1))


def _proj_kernel(pos_ref, x_ref, ws_ref, wuq_ref, wukv_ref, gq_ref, gkv_ref, gaq_ref, gak_ref,
                 bd_ref, tab_ref,
                 qm_ref, km_ref, vm_ref, wq_ref, wk_ref, wv_ref, aq_ref, ak_ref, av_ref):
    del pos_ref
    tm = x_ref.shape[0]
    x = x_ref[...].astype(BF16)
    res = _dot(x, ws_ref[...])

    m_cos = tab_ref[:, 0 * LANES:1 * LANES]
    m_sa = tab_ref[:, 1 * LANES:2 * LANES]
    m_sb = tab_ref[:, 2 * LANES:3 * LANES]
    a_cos = tab_ref[:, 3 * LANES:4 * LANES]
    a_sa = tab_ref[:, 4 * LANES:5 * LANES]
    a_sb = tab_ref[:, 5 * LANES:6 * LANES]
    c_q = _rms_rows(res[:, C_CQ:C_CKV], gq_ref[...]).astype(BF16)
    q = _dot(c_q, wuq_ref[...])
    q_scale = MLA_QK ** -0.5 * LOG2E
    for h in range(MLA_HEADS):
        base = h * MLA_QK_PAD
        qm_ref[:, base:base + LANES] = (q[:, base:base + LANES] * q_scale).astype(BF16)
        roped = _rope_chunk(q[:, base + LANES:base + 2 * LANES], m_cos, m_sa, m_sb, MLA_ROPE // 2)
        qm_ref[:, base + LANES:base + 2 * LANES] = (roped * q_scale).astype(BF16)

    c_kv = _rms_rows(res[:, C_CKV:C_KR], gkv_ref[...]).astype(BF16)
    kv = _dot(c_kv, wukv_ref[...])
    k_rope = _rope_chunk(res[:, C_KR:C_WQ], m_cos, m_sa, m_sb, MLA_ROPE // 2).astype(BF16)
    for h in range(MLA_HEADS):
        base = h * MLA_QK_PAD
        km_ref[:, base:base + LANES] = kv[:, h * LANES:(h + 1) * LANES].astype(BF16)
        km_ref[:, base + LANES:base + 2 * LANES] = k_rope
    v_off = MLA_HEADS * MLA_NOPE
    for r in range(tm // ATT_TK):
        rows = slice(r * ATT_TK, (r + 1) * ATT_TK)
        for h in range(MLA_HEADS):
            vm_ref[r, h * MLA_V:(h + 1) * MLA_V, :] = (
                kv[rows, v_off + h * MLA_V:v_off + (h + 1) * MLA_V].T.astype(BF16))
        av_ref[r] = res[rows, C_AV:C_END].T.astype(BF16)

    wq_ref[...] = (res[:, C_WQ:C_WK] * (HEAD_DIM ** -0.5 * LOG2E)).astype(BF16)
    wk_ref[...] = res[:, C_WK:C_WV].astype(BF16)
    for r in range(tm // Q_BLOCK):
        wv_ref[r] = res[r * Q_BLOCK:(r + 1) * Q_BLOCK, C_WV:C_AQ].T.astype(BF16)

    bd = bd_ref[...]

    def norm_rope(xc, g):
        ms = _dot((xc * xc).astype(BF16), bd)
        xn = xc * lax.rsqrt(ms + RMS_EPS) * g
        return _rope_chunk(xn, a_cos, a_sa, a_sb, HEAD_DIM // 4)

    for c in range(QW // LANES):
        xc = res[:, C_AQ + c * LANES:C_AQ + (c + 1) * LANES]
        aq_ref[:, c * LANES:(c + 1) * LANES] = (
            norm_rope(xc, gaq_ref[...]) * (HEAD_DIM ** -0.5 * LOG2E)).astype(BF16)
    ak_ref[...] = norm_rope(res[:, C_AK:C_AV], gak_ref[...]).astype(BF16)


def _in_proj(x, pos_blk, lw, tab, tm):
    n = x.shape[0]
    const = lambda shape: pl.BlockSpec(shape, lambda i, p: (0, 0))
    rows = lambda w: pl.BlockSpec((tm, w), lambda i, p: (i, 0))
    out_widths = [MLA_HEADS * MLA_QK_PAD, MLA_HEADS * MLA_QK_PAD, (MLA_HEADS * MLA_V, ATT_TK),
                  QW, KW, (KW, Q_BLOCK), QW, KW, (KW, ATT_TK)]

    def out_spec(w):
        if isinstance(w, tuple):
            return pl.BlockSpec((tm // w[1], w[0], w[1]), lambda i, p: (i, 0, 0))
        return rows(w)

    def out_shape(w):
        if isinstance(w, tuple):
            return jax.ShapeDtypeStruct((n // w[1], w[0], w[1]), BF16)
        return jax.ShapeDtypeStruct((n, w), BF16)

    grid_spec = pltpu.PrefetchScalarGridSpec(
        num_scalar_prefetch=1,
        grid=(n // tm,),
        in_specs=[rows(D_MODEL),
                  const((D_MODEL, C_END)),
                  const((MLA_Q_RANK, MLA_HEADS * MLA_QK_PAD)),
                  const((MLA_KV_RANK, 2 * MLA_HEADS * LANES)),
                  const((1, MLA_Q_RANK)), const((1, MLA_KV_RANK)),
                  const((1, LANES)), const((1, LANES)),
                  const((LANES, LANES)),
                  pl.BlockSpec((tm, TAB_W), lambda i, p: (p[i], 0))],
        out_specs=[out_spec(w) for w in out_widths],
    )
    return pl.pallas_call(
        _proj_kernel,
        grid_spec=grid_spec,
        out_shape=[out_shape(w) for w in out_widths],
        compiler_params=_cparams(("parallel",)),
        name="in_proj",
    )(pos_blk, x, lw["w_s"], lw["w_uq"], lw["w_ukv"], lw["g_q"], lw["g_kv"], lw["g_aq"], lw["g_ak"],
      lw["bd"], tab)


def _call_into(prev, kernel, operands, *, in_specs, **kwargs):
    if prev is None:
        prev = jnp.zeros(kwargs["out_shape"].shape, kwargs["out_shape"].dtype)
    n_in = len(operands)

    def skip_prev(*refs):
        return kernel(*refs[:n_in], *refs[n_in + 1:])

    return pl.pallas_call(skip_prev, in_specs=list(in_specs) + [pl.BlockSpec(memory_space=pl.ANY)],
                          input_output_aliases={n_in: 0}, **kwargs)(*operands, prev)


def _flash_t(qt_sc, k_ref, vt_ref, s0_sc, s1_sc, acc_sc, seq, v_rows):
    n_blk = acc_sc.shape[0]
    nk = seq // ATT_TK
    assert nk >= 2 and nk % 2 == 0
    acc_sc[...] = jnp.zeros(acc_sc.shape, F32)
    stat = lambda v: tuple(jnp.full((1, MXU_DIM), v, F32) for _ in range(n_blk))

    def scores(j, s_ref):
        k = k_ref[pl.ds(pl.multiple_of(j * ATT_TK, ATT_TK), ATT_TK), :]
        tops = []
        for b in range(n_blk):
            s = _dot(k, qt_sc[:, b * MXU_DIM:(b + 1) * MXU_DIM])
            s_ref[b] = s
            tops.append(jnp.max(s, axis=0, keepdims=True))
        return tuple(tops)

    ones_rows = jnp.where(lax.broadcasted_iota(jnp.int32, (BF16_ROWS, ATT_TK), 0) == 0, 1.0, 0.0).astype(BF16)

    def consume(j, s_ref, tops, ms):
        ms = list(ms)
        for b in range(n_blk):
            m_new = jnp.maximum(ms[b], tops[b])
            alpha = jnp.exp2(ms[b] - m_new)
            p = jnp.exp2((s_ref[b] - m_new).astype(BF16))
            ms[b] = m_new
            r0, r1 = v_rows[b]
            vt = jnp.concatenate([vt_ref[j, r0:r1, :], ones_rows], axis=0)
            acc_sc[b] = acc_sc[b] * alpha + _dot(vt, p)
        return tuple(ms)

    def pair(i, carry):
        ms, tops0 = carry
        tops1 = scores(2 * i + 1, s1_sc)
        ms = consume(2 * i, s0_sc, tops0, ms)
        tops0 = scores(2 * i + 2, s0_sc)
        return consume(2 * i + 1, s1_sc, tops1, ms), tops0

    trips = nk // 2 - 1
    unroll = trips if trips <= 5 else next(u for u in (5, 4, 3, 2, 1) if trips % u == 0)
    ms, tops0 = lax.fori_loop(0, trips, pair, (stat(NEG_INF), scores(0, s0_sc)), unroll=max(unroll, 1))
    tops1 = scores(nk - 1, s1_sc)
    ms = consume(nk - 2, s0_sc, tops0, ms)
    consume(nk - 1, s1_sc, tops1, ms)


def _stage_queries(q_ref, qt_sc):
    tq = q_ref.shape[0]
    lo = lax.broadcasted_iota(jnp.int32, (LANES, tq), 0) < HEAD_DIM
    for c in range(GROUP):
        qt = q_ref[:, c * LANES:(c + 1) * LANES].astype(F32).T
        qt_sc[:, c * tq:(c + 1) * tq] = jnp.where(lo, qt, 0.0).astype(BF16)
        qt_sc[:, (GROUP + c) * tq:(GROUP + c + 1) * tq] = jnp.where(lo, 0.0, qt).astype(BF16)


def _store_heads(head_out, o_ref):
    for c in range(GROUP):
        ot = jnp.concatenate([head_out(c), head_out(GROUP + c)], axis=0)
        o_ref[:, c * LANES:(c + 1) * LANES] = ot.T.astype(o_ref.dtype)


AXIAL_TQ = 128
AXIAL_SWEEPS = 4


def _gqa_kernel(q_ref, k_ref, vt_ref, o_ref, *scratch, seq):
    tq = AXIAL_TQ
    heads_per_blk = MXU_DIM // tq
    for t in range(AXIAL_SWEEPS):
        qt_sc, s0_sc, s1_sc, acc_sc = scratch[4 * t:4 * t + 4]
        rows = pl.ds(t * tq, tq)
        _stage_queries(q_ref.at[rows], qt_sc)
        v_rows = [((b * heads_per_blk) // GROUP * HEAD_DIM, ((b * heads_per_blk) // GROUP + 1) * HEAD_DIM)
                  for b in range(acc_sc.shape[0])]
        _flash_t(qt_sc, k_ref, vt_ref, s0_sc, s1_sc, acc_sc, seq, v_rows)

        def head_out(h, acc_sc=acc_sc):
            b, off = divmod(h * tq, MXU_DIM)
            return acc_sc[b, 0:HEAD_DIM, off:off + tq] / acc_sc[b, HEAD_DIM:HEAD_DIM + 1, off:off + tq]

        _store_heads(head_out, o_ref.at[rows])


def _axial_attention(q, k, vt, prev, tok_off, batch, seq):
    tq = AXIAL_TQ * AXIAL_SWEEPS
    nq = seq // tq
    nkb = seq // ATT_TK
    qb, sb = tok_off // tq, tok_off // seq
    n_blk = HEADS * AXIAL_TQ // MXU_DIM
    sweep_scratch = [pltpu.VMEM((KW, HEADS * AXIAL_TQ), BF16),
                     pltpu.VMEM((n_blk, ATT_TK, MXU_DIM), F32),
                     pltpu.VMEM((n_blk, ATT_TK, MXU_DIM), F32),
                     pltpu.VMEM((n_blk, HEAD_DIM + BF16_ROWS, MXU_DIM), F32)]
    return _call_into(
        prev, functools.partial(_gqa_kernel, seq=seq), (q, k, vt),
        grid=(batch, nq),
        in_specs=[pl.BlockSpec((tq, QW), lambda b, i: (qb + b * nq + i, 0)),
                  pl.BlockSpec((seq, KW), lambda b, i: (sb + b, 0)),
                  pl.BlockSpec((nkb, KW, ATT_TK), lambda b, i: (sb + b, 0, 0))],
        out_specs=pl.BlockSpec((tq, QW), lambda b, i: (qb + b * nq + i, 0)),
        out_shape=jax.ShapeDtypeStruct((q.shape[0], QW), BF16),
        scratch_shapes=sweep_scratch * AXIAL_SWEEPS,
        compiler_params=_cparams(("parallel", "arbitrary")),
        name="axial_attention",
    )


MLA_SWEEP_BLOCKS = 4
MLA_SWEEPS = 2


def _mla_kernel(q_ref, k_ref, vt_ref, o_ref, *scratch, seq):
    sweeps = len(scratch) // 4
    n_blk = q_ref.shape[0] // (sweeps * MXU_DIM)
    for t in range(sweeps):
        qt_sc, s0_sc, s1_sc, acc_sc = scratch[4 * t:4 * t + 4]
        for b in range(n_blk):
            rows = pl.ds((t * n_blk + b) * MXU_DIM, MXU_DIM)
            qt_sc[:, b * MXU_DIM:(b + 1) * MXU_DIM] = q_ref[rows, :].astype(F32).T.astype(BF16)
        _flash_t(qt_sc, k_ref, vt_ref, s0_sc, s1_sc, acc_sc, seq, [(0, MLA_V)] * n_blk)
        for b in range(n_blk):
            rows = pl.ds((t * n_blk + b) * MXU_DIM, MXU_DIM)
            o = acc_sc[b, 0:MLA_V, :] / acc_sc[b, MLA_V:MLA_V + 1, :]
            o_ref[rows, :] = o.T.astype(o_ref.dtype)


def _mla_attention(q, k, vt, prev, tok_off, batch, seq):
    tq = _pick_tile(seq, MLA_SWEEPS * MLA_SWEEP_BLOCKS * MXU_DIM)
    sweeps = MLA_SWEEPS if tq % (MLA_SWEEPS * MXU_DIM) == 0 else 1
    n_blk = tq // (sweeps * MXU_DIM)
    assert n_blk * sweeps * MXU_DIM == tq
    nq = seq // tq
    nkb = seq // ATT_TK
    qb, sb = tok_off // tq, tok_off // seq
    sweep_scratch = [pltpu.VMEM((MLA_QK_PAD, n_blk * MXU_DIM), BF16),
                     pltpu.VMEM((n_blk, ATT_TK, MXU_DIM), F32),
                     pltpu.VMEM((n_blk, ATT_TK, MXU_DIM), F32),
                     pltpu.VMEM((n_blk, MLA_V + BF16_ROWS, MXU_DIM), F32)]
    return _call_into(
        prev, functools.partial(_mla_kernel, seq=seq), (q, k, vt),
        grid=(batch, MLA_HEADS, nq),
        in_specs=[pl.BlockSpec((tq, MLA_QK_PAD), lambda b, h, i: (qb + b * nq + i, h)),
                  pl.BlockSpec((seq, MLA_QK_PAD), lambda b, h, i: (sb + b, h)),
                  pl.BlockSpec((nkb, MLA_V, ATT_TK), lambda b, h, i: (sb + b, h, 0))],
        out_specs=pl.BlockSpec((tq, MLA_V), lambda b, h, i: (qb + b * nq + i, h)),
        out_shape=jax.ShapeDtypeStruct((q.shape[0], MLA_HEADS * MLA_V), BF16),
        scratch_shapes=sweep_scratch * sweeps,
        compiler_params=_cparams(("parallel", "parallel", "arbitrary")),
        name="mla_attention",
    )


WINDOW_SWEEPS = 8


def _window_bias():
    key = jnp.arange(SPAN, dtype=jnp.int32)[:, None]
    qry = jnp.arange(Q_BLOCK, dtype=jnp.int32)[None, :]
    slopes = jnp.array([2.0 ** (-8.0 * (h + 1) / HEADS) * LOG2E for h in range(HEADS)], F32)
    tabs = []
    for lead in (0, WINDOW, 2 * WINDOW):
        dist = jnp.abs(key - lead - qry)
        tabs.append(jnp.where(dist <= WINDOW, -slopes[:, None, None] * dist.astype(F32)[None], NEG_INF))
    return jnp.stack(tabs)


def _window_kernel(sink_ref, bias_ref, q_ref, k_ref, vt_ref, o_ref, *qt_scs, seq):
    for t, qt_sc in enumerate(qt_scs):
        rows = pl.ds(t * Q_BLOCK, Q_BLOCK)
        _window_block(sink_ref, bias_ref, q_ref.at[rows], k_ref, vt_ref, o_ref.at[rows], qt_sc,
                      (pl.program_id(1) * len(qt_scs) + t) * Q_BLOCK, seq)


def _window_block(sink_ref, bias_ref, q_ref, k_ref, vt_ref, o_ref, qt_sc, start, seq):
    kstart = pl.multiple_of(jnp.clip(start - WINDOW, 0, seq - SPAN), Q_BLOCK)
    kb = kstart // Q_BLOCK
    lead = (start - kstart) // WINDOW
    _stage_queries(q_ref, qt_sc)
    s_all = _dot(k_ref[pl.ds(kstart, SPAN), :], qt_sc[...])
    vt = jnp.concatenate([vt_ref[kb + j] for j in range(SPAN // Q_BLOCK)], axis=1)
    ones_rows = jnp.where(lax.broadcasted_iota(jnp.int32, (BF16_ROWS, SPAN), 0) == 0, 1.0, 0.0).astype(BF16)
    outs = []
    for blk in range(HEADS // 2):
        ps, sinks = [], []
        for h in (2 * blk, 2 * blk + 1):
            sink = sink_ref[h] * LOG2E
            logits = s_all[:, h * Q_BLOCK:(h + 1) * Q_BLOCK] + bias_ref[lead, h]
            m = jnp.maximum(jnp.max(logits, axis=0, keepdims=True), sink)
            ps.append(jnp.exp2((logits - m).astype(BF16)))
            sinks.append(jnp.exp2(sink - m))
        g = (2 * blk) // GROUP
        lhs = jnp.concatenate([vt[g * HEAD_DIM:(g + 1) * HEAD_DIM], ones_rows], axis=0)
        pv = _dot(lhs, jnp.concatenate(ps, axis=1))
        denom = pv[HEAD_DIM:HEAD_DIM + 1] + jnp.concatenate(sinks, axis=1)
        outs.append(pv[0:HEAD_DIM] / denom)
    _store_heads(lambda h: outs[h // 2][:, (h % 2) * Q_BLOCK:(h % 2 + 1) * Q_BLOCK], o_ref)


def _window_attention(q, k, v, sink, bias, prev, tok_off, batch, seq):
    tq = _pick_tile(seq, WINDOW_SWEEPS * Q_BLOCK)
    assert seq >= SPAN + Q_BLOCK and tq % Q_BLOCK == 0
    nq = seq // tq
    qb, sb = tok_off // tq, tok_off // seq
    return _call_into(
        prev, functools.partial(_window_kernel, seq=seq), (sink, bias, q, k, v),
        grid=(batch, nq),
        in_specs=[pl.BlockSpec(memory_space=pltpu.SMEM),
                  pl.BlockSpec(bias.shape, lambda b, i: (0, 0, 0, 0)),
                  pl.BlockSpec((tq, QW), lambda b, i: (qb + b * nq + i, 0)),
                  pl.BlockSpec((seq, KW), lambda b, i: (sb + b, 0)),
                  pl.BlockSpec((seq // Q_BLOCK, KW, Q_BLOCK), lambda b, i: (sb + b, 0, 0))],
        out_specs=pl.BlockSpec((tq, QW), lambda b, i: (qb + b * nq + i, 0)),
        out_shape=jax.ShapeDtypeStruct((q.shape[0], QW), BF16),
        scratch_shapes=[pltpu.VMEM((KW, HEADS * Q_BLOCK), BF16)] * (tq // Q_BLOCK),
        compiler_params=_cparams(("parallel", "arbitrary")),
        name="window_attention",
    )


def _route(x, w_hi, w_lo):
    hi = x.astype(BF16)
    lo = (x - hi.astype(F32)).astype(BF16)
    logits = _dot(hi, w_hi) + _dot(lo, w_hi) + _dot(hi, w_lo)
    lane = lax.broadcasted_iota(jnp.int32, logits.shape, 1)
    logits = jnp.where(lane < N_EXPERTS, logits, NEG_INF)
    t1 = jnp.max(logits, axis=1, keepdims=True)
    e1 = jnp.min(jnp.where(logits == t1, lane, LANES), axis=1, keepdims=True)
    rest = jnp.where(lane == e1, NEG_INF, logits)
    t2 = jnp.max(rest, axis=1, keepdims=True)
    e2 = jnp.min(jnp.where(rest == t2, lane, LANES), axis=1, keepdims=True)
    w = jnp.exp(t2 - t1)
    g1 = 1.0 / (1.0 + w)
    g2 = w / (1.0 + w)
    return (jnp.where(lane == 0, e1, jnp.where(lane == 1, e2, 0)),
            jnp.where(lane == 0, g1, jnp.where(lane == 1, g2, 0.0)))


def _router_kernel(x_ref, whi_ref, wlo_ref, e_ref, gate_ref):
    e_ref[...], gate_ref[...] = _route(x_ref[...], whi_ref[...], wlo_ref[...])


def _router(x, w_hi, w_lo, tm):
    n = x.shape[0]
    return pl.pallas_call(
        _router_kernel,
        grid=(n // tm,),
        in_specs=[pl.BlockSpec((tm, D_MODEL), lambda i: (i, 0)),
                  pl.BlockSpec((D_MODEL, LANES), lambda i: (0, 0)),
                  pl.BlockSpec((D_MODEL, LANES), lambda i: (0, 0))],
        out_specs=[pl.BlockSpec((tm, LANES), lambda i: (i, 0)),
                   pl.BlockSpec((tm, LANES), lambda i: (i, 0))],
        out_shape=[jax.ShapeDtypeStruct((n, LANES), jnp.int32),
                   jax.ShapeDtypeStruct((n, LANES), F32)],
        compiler_params=_cparams(("parallel",)),
        name="moe_router",
    )(x, w_hi, w_lo)


def _merge_kernel(x_ref, oa_ref, ob_ref, oc_ref, wg_ref, wa_ref, wb_ref, wc_ref, wo_ref, g_ref, b_ref,
                  o_ref, *, alpha):
    x = x_ref[...]
    xb = x.astype(BF16)
    merged = None
    for idx, (o_br, w_br) in enumerate(((oa_ref, wa_ref), (ob_ref, wb_ref), (oc_ref, wc_ref))):
        gate = jax.nn.sigmoid(_dot(xb, wg_ref[:, idx * D_MODEL:(idx + 1) * D_MODEL]))
        term = gate * _dot(o_br[...], w_br[...])
        merged = term if merged is None else merged + term
    m = _dot(merged.astype(BF16), wo_ref[...])
    o_ref[...] = _ln_rows(alpha * x + m, g_ref[...], b_ref[...])


def _merge(x, o_a, o_b, o_c, lw, alpha, tm):
    n = x.shape[0]
    const = lambda shape: pl.BlockSpec(shape, lambda i: (0, 0))
    rows = lambda w: pl.BlockSpec((tm, w), lambda i: (i, 0))
    return pl.pallas_call(
        functools.partial(_merge_kernel, alpha=alpha),
        grid=(n // tm,),
        in_specs=[rows(D_MODEL), rows(MLA_HEADS * MLA_V), rows(QW), rows(QW),
                  const((D_MODEL, N_BRANCH * D_MODEL)),
                  const((MLA_HEADS * MLA_V, D_MODEL)), const((QW, D_MODEL)), const((QW, D_MODEL)),
                  const((D_MODEL, D_MODEL)), const((1, D_MODEL)), const((1, D_MODEL))],
        out_specs=rows(D_MODEL),
        out_shape=jax.ShapeDtypeStruct((n, D_MODEL), F32),
        compiler_params=_cparams(("parallel",)),
        name="gated_merge",
    )(x, o_a, o_b, o_c, lw["w_g"], lw["w_br_mla"], lw["w_br_win"], lw["w_br_ax"], lw["w_out"],
      lw["ln1_g"], lw["ln1_b"])


def _ffn_kernel(x_ref, w1_ref, w3_ref, w2_ref, g_ref, b_ref, o_ref, acc_ref, *, alpha):
    f = pl.program_id(1)

    @pl.when(f == 0)
    def _():
        acc_ref[...] = jnp.zeros(acc_ref.shape, F32)

    xb = x_ref[...].astype(BF16)
    h = jax.nn.silu(_dot(xb, w1_ref[...])) * _dot(xb, w3_ref[...])
    acc_ref[...] += _dot(h.astype(BF16), w2_ref[...])

    @pl.when(f == pl.num_programs(1) - 1)
    def _():
        o_ref[...] = _ln_rows(alpha * x_ref[...] + acc_ref[...], g_ref[...], b_ref[...])


def _dense_ffn(x, w1, w3, w2, g, b, alpha, tm, tf):
    n = x.shape[0]
    d_ff = w1.shape[1]
    mode = dict(pipeline_mode=pl.Buffered(1)) if tf == d_ff else {}
    return pl.pallas_call(
        functools.partial(_ffn_kernel, alpha=alpha),
        grid=(n // tm, d_ff // tf),
        in_specs=[pl.BlockSpec((tm, D_MODEL), lambda i, f: (i, 0)),
                  pl.BlockSpec((D_MODEL, tf), lambda i, f: (0, f), **mode),
                  pl.BlockSpec((D_MODEL, tf), lambda i, f: (0, f), **mode),
                  pl.BlockSpec((tf, D_MODEL), lambda i, f: (f, 0), **mode),
                  pl.BlockSpec((1, D_MODEL), lambda i, f: (0, 0)),
                  pl.BlockSpec((1, D_MODEL), lambda i, f: (0, 0))],
        out_specs=pl.BlockSpec((tm, D_MODEL), lambda i, f: (i, 0)),
        out_shape=jax.ShapeDtypeStruct((n, D_MODEL), F32),
        scratch_shapes=[pltpu.VMEM((tm, D_MODEL), F32)],
        compiler_params=_cparams(("parallel", "arbitrary")),
        name="dense_ffn",
    )(x, w1, w3, w2, g, b)


ROW_TILE = 8


def _row_of(ref3, r):
    return ref3.at[lax.shift_right_logical(r, 3), pl.ds(jnp.bitwise_and(r, ROW_TILE - 1), 1)]


def _dispatch_kernel(pad_ref, dest_hbm, x_ref, xs_hbm, dest_smem, zero_ref, idx_sem, row_sem, *, n_pad, zpad):
    i = pl.program_id(0)
    groups = x_ref.shape[0]
    n_idx = TOP_K * ROW_TILE * groups
    cp = pltpu.make_async_copy(dest_hbm.at[pl.ds(pl.multiple_of(i * n_idx, n_idx), n_idx)], dest_smem, idx_sem)
    cp.start()
    zero_ref[...] = jnp.zeros(zero_ref.shape, F32)
    cp.wait()

    def send(g, carry):
        for u in range(ROW_TILE):
            for k in range(TOP_K):
                d = dest_smem[(g * ROW_TILE + u) * TOP_K + k]
                pltpu.make_async_copy(x_ref.at[g, pl.ds(u, 1)], _row_of(xs_hbm, d), row_sem).start(priority=k)
        return carry

    lax.fori_loop(0, groups, send, 0)

    def pad_copy(r):
        return pltpu.make_async_copy(zero_ref.at[pl.ds(0, 1)], _row_of(xs_hbm, pad_ref[i * zpad + r]), row_sem)

    def send_zero(r, carry):
        @pl.when(i * zpad + r < n_pad)
        def _():
            pad_copy(r).start()
        return carry

    def drain_zero(r, carry):
        @pl.when(i * zpad + r < n_pad)
        def _():
            pad_copy(r).wait()
        return carry

    def drain(r, carry):
        pltpu.make_async_copy(x_ref.at[0, pl.ds(0, 1)], xs_hbm.at[0, pl.ds(0, 1)], row_sem).wait()
        return carry

    lax.fori_loop(0, zpad, send_zero, 0)
    lax.fori_loop(0, n_idx, drain, 0, unroll=8)
    lax.fori_loop(0, zpad, drain_zero, 0)


def _dispatch_rows(x, dest, pad_slots, p, tm):
    n = x.shape[0]
    steps = n // tm
    n_pad = pad_slots.shape[0]
    zpad = -(-n_pad // steps)
    grid_spec = pltpu.PrefetchScalarGridSpec(
        num_scalar_prefetch=1,
        grid=(steps,),
        in_specs=[pl.BlockSpec(memory_space=pl.ANY),
                  pl.BlockSpec((tm // ROW_TILE, ROW_TILE, D_MODEL), lambda i, pad: (i, 0, 0))],
        out_specs=pl.BlockSpec(memory_space=pl.ANY),
        scratch_shapes=[pltpu.SMEM((TOP_K * tm,), jnp.int32),
                        pltpu.VMEM((ROW_TILE, D_MODEL), F32),
                        pltpu.SemaphoreType.DMA, pltpu.SemaphoreType.DMA],
    )
    xs = pl.pallas_call(
        functools.partial(_dispatch_kernel, n_pad=n_pad, zpad=zpad),
        grid_spec=grid_spec,
        out_shape=jax.ShapeDtypeStruct((p // ROW_TILE, ROW_TILE, D_MODEL), F32),
        compiler_params=_cparams(("arbitrary",)),
        name="moe_dispatch",
    )(pad_slots, dest, x.reshape(n // ROW_TILE, ROW_TILE, D_MODEL))
    return xs.reshape(p, D_MODEL)


def _expert_kernel(be_ref, bv_ref, x_ref, w1_ref, w3_ref, w2_ref, o_ref, acc_ref):
    del be_ref
    i = pl.program_id(0)
    f = pl.program_id(1)
    valid = bv_ref[i] != 0

    @pl.when(f == 0)
    def _():
        acc_ref[...] = jnp.zeros(acc_ref.shape, F32)

    @pl.when(valid)
    def _():
        xb = x_ref[...].astype(BF16)
        h = jax.nn.silu(_dot(xb, w1_ref[0])) * _dot(xb, w3_ref[0])
        acc_ref[...] += _dot(h.astype(BF16), w2_ref[0])

    @pl.when(f == pl.num_programs(1) - 1)
    def _():
        o_ref[...] = acc_ref[...]


def _expert_ffn(xs, blk_expert, blk_valid, w1, w3, w2, tm, tf):
    p = xs.shape[0]
    d_ff = w1.shape[2]
    nf = d_ff // tf
    fidx = lambda i, f, be, bv: jnp.where(bv[i] != 0, f, nf - 1)
    grid_spec = pltpu.PrefetchScalarGridSpec(
        num_scalar_prefetch=2,
        grid=(p // tm, nf),
        in_specs=[pl.BlockSpec((tm, D_MODEL), lambda i, f, be, bv: (i, 0)),
                  pl.BlockSpec((1, D_MODEL, tf), lambda i, f, be, bv: (be[i], 0, fidx(i, f, be, bv))),
                  pl.BlockSpec((1, D_MODEL, tf), lambda i, f, be, bv: (be[i], 0, fidx(i, f, be, bv))),
                  pl.BlockSpec((1, tf, D_MODEL), lambda i, f, be, bv: (be[i], fidx(i, f, be, bv), 0))],
        out_specs=pl.BlockSpec((tm, D_MODEL), lambda i, f, be, bv: (i, 0)),
        scratch_shapes=[pltpu.VMEM((tm, D_MODEL), F32)],
    )
    return pl.pallas_call(
        _expert_kernel,
        grid_spec=grid_spec,
        out_shape=jax.ShapeDtypeStruct((p, D_MODEL), F32),
        compiler_params=_cparams(("arbitrary", "arbitrary")),
        name="moe_experts",
    )(blk_expert, blk_valid, xs, w1, w3, w2)


def _combine_kernel(pos_hbm, ys_hbm, x_ref, gate_ref, g_ref, b_ref, o_ref,
                    r0_ref, r1_ref, pos_smem, pos_sem, row_sem, *, alpha):
    i = pl.program_id(0)
    tm = x_ref.shape[0]
    n_idx = TOP_K * tm
    cp = pltpu.make_async_copy(pos_hbm.at[pl.ds(pl.multiple_of(i * n_idx, n_idx), n_idx)], pos_smem, pos_sem)
    cp.start()
    cp.wait()

    def fetch(g, carry):
        for u in range(ROW_TILE):
            for k, buf in enumerate((r0_ref, r1_ref)):
                src = _row_of(ys_hbm, pos_smem[(g * ROW_TILE + u) * TOP_K + k])
                pltpu.make_async_copy(src, buf.at[g, pl.ds(u, 1)], row_sem).start(priority=k)
        return carry

    def drain(r, carry):
        pltpu.make_async_copy(ys_hbm.at[0, pl.ds(0, 1)], r0_ref.at[0, pl.ds(0, 1)], row_sem).wait()
        return carry

    lax.fori_loop(0, tm // ROW_TILE, fetch, 0)
    lax.fori_loop(0, n_idx, drain, 0, unroll=8)
    gates = gate_ref[...]
    y = (gates[:, 0:1] * r0_ref[...].reshape(tm, D_MODEL) + gates[:, 1:2] * r1_ref[...].reshape(tm, D_MODEL))
    o_ref[...] = _ln_rows(alpha * x_ref[...] + y, g_ref[...], b_ref[...])


def _moe_combine(x, ys, pos, gates, g, b, alpha, tm):
    n = x.shape[0]
    return pl.pallas_call(
        functools.partial(_combine_kernel, alpha=alpha),
        grid=(n // tm,),
        in_specs=[pl.BlockSpec(memory_space=pl.ANY), pl.BlockSpec(memory_space=pl.ANY),
                  pl.BlockSpec((tm, D_MODEL), lambda i: (i, 0)),
                  pl.BlockSpec((tm, LANES), lambda i: (i, 0)),
                  pl.BlockSpec((1, D_MODEL), lambda i: (0, 0)),
                  pl.BlockSpec((1, D_MODEL), lambda i: (0, 0))],
        out_specs=pl.BlockSpec((tm, D_MODEL), lambda i: (i, 0)),
        out_shape=jax.ShapeDtypeStruct((n, D_MODEL), F32),
        scratch_shapes=[pltpu.VMEM((tm // ROW_TILE, ROW_TILE, D_MODEL), F32),
                        pltpu.VMEM((tm // ROW_TILE, ROW_TILE, D_MODEL), F32),
                        pltpu.SMEM((TOP_K * tm,), jnp.int32),
                        pltpu.SemaphoreType.DMA, pltpu.SemaphoreType.DMA],
        compiler_params=_cparams(("arbitrary",)),
        name="moe_combine",
    )(pos, ys.reshape(ys.shape[0] // ROW_TILE, ROW_TILE, D_MODEL), x, gates, g, b)


def _moe_layer(x, e_out, gates, w1, w3, w2, g, b, alpha, tm, tm_e, tf):
    n = x.shape[0]
    a = n * TOP_K
    e_flat = e_out[:, :TOP_K].reshape(a)
    onehot = (e_flat[:, None] == jnp.arange(N_EXPERTS, dtype=jnp.int32)[None, :]).astype(jnp.int32)
    csum = jnp.cumsum(onehot, axis=0)
    counts = csum[-1]
    rank = jnp.sum((csum - 1) * onehot, axis=1)
    padded = (counts + tm_e - 1) // tm_e * tm_e
    pad_end = jnp.cumsum(padded)
    pad_start = pad_end - padded
    dest = (pad_start[e_flat] + rank).astype(jnp.int32)
    assert a % tm_e == 0
    n_pad = N_EXPERTS * tm_e
    p = a + n_pad
    npad = padded - counts
    pad_cum = jnp.cumsum(npad)
    pad_i = jnp.arange(n_pad, dtype=jnp.int32)
    pad_e = jnp.searchsorted(pad_cum, pad_i, side="right")
    pad_ec = jnp.minimum(pad_e, N_EXPERTS - 1)
    pad_slots = jnp.where(pad_e >= N_EXPERTS, pad_end[-1] + (pad_i - pad_cum[-1]),
                          (pad_start + counts)[pad_ec] + (pad_i - (pad_cum - npad)[pad_ec])).astype(jnp.int32)
    blk_start = jnp.arange(p // tm_e, dtype=jnp.int32) * tm_e
    blk_valid = (blk_start < pad_end[-1]).astype(jnp.int32)
    blk_expert = jnp.minimum(jnp.searchsorted(pad_end, jnp.minimum(blk_start, pad_end[-1] - 1), side="right"),
                             N_EXPERTS - 1).astype(jnp.int32)

    xs = _dispatch_rows(x, dest, pad_slots, p, tm)
    ys = _expert_ffn(xs, blk_expert, blk_valid, w1, w3, w2, tm_e, tf)
    return _moe_combine(x, ys, dest, gates, g, b, alpha, tm)


def _rope_tables(t_max):
    t = jnp.arange(t_max, dtype=jnp.int32)

    def freqs(pos, dim):
        inv = ROPE_THETA ** (-jnp.arange(0, dim, 2, dtype=F32) / dim)
        ang = pos.astype(F32)[:, None] * inv[None, :]
        return jnp.cos(ang), jnp.sin(ang)

    zeros = lambda w: jnp.zeros((t_max, w), F32)
    c, s = freqs(t, MLA_ROPE)
    half = MLA_ROPE // 2
    m_cos = jnp.concatenate([c, c, zeros(LANES - MLA_ROPE)], axis=1)
    m_sa = jnp.concatenate([-s, zeros(LANES - half)], axis=1)
    m_sb = jnp.concatenate([zeros(half), s, zeros(LANES - MLA_ROPE)], axis=1)
    cr, sr = freqs(t // GRID_W, HEAD_DIM // 2)
    cc, sc = freqs(t % GRID_W, HEAD_DIM // 2)
    q = HEAD_DIM // 4
    head_cos = jnp.concatenate([cr, cr, cc, cc], axis=1)
    head_sa = jnp.concatenate([-sr, zeros(q), -sc, zeros(q)], axis=1)
    head_sb = jnp.concatenate([zeros(q), sr, zeros(q), sc], axis=1)
    rep = lambda x: jnp.concatenate([x] * (LANES // HEAD_DIM), axis=1)
    return jnp.concatenate([m_cos, m_sa, m_sb, rep(head_cos), rep(head_sa), rep(head_sb)], axis=1)


def _chunk_heads(w, axis):
    shape = w.shape
    w = w.reshape(shape[:axis] + (HEADS, HEAD_DIM) + shape[axis + 1:])
    w = jnp.take(w, jnp.array(HEAD_PERM), axis=axis)
    return w.reshape(shape)


def _prep_layer(l, w_in, mla_q_norm, mla_kv_norm, w_uq, w_ukv, ax_q_norm, ax_k_norm,
                w_br_mla, w_br_win, w_br_ax, w_out, ln1_g, ln1_b, ln2_g, ln2_b):
    wi = w_in[l]
    sizes = [MLA_Q_RANK, MLA_KV_RANK, MLA_ROPE, QW, KW, KW, QW, KW, KW, N_BRANCH * D_MODEL]
    offs = [0]
    for s in sizes:
        offs.append(offs[-1] + s)
    cols = [wi[:, offs[j]:offs[j + 1]] for j in range(len(sizes))]
    c_q, c_kv, k_r, wq, wk, wv, aq, ak, av, wg = cols
    w_s = jnp.concatenate([c_q, c_kv, k_r, jnp.zeros((D_MODEL, LANES - MLA_ROPE), F32),
                           _chunk_heads(wq, 1), wk, wv, _chunk_heads(aq, 1), ak, av], axis=1).astype(BF16)
    uq = w_uq[l].reshape(MLA_Q_RANK, MLA_HEADS, MLA_QK)
    uq = jnp.pad(uq, ((0, 0), (0, 0), (0, MLA_QK_PAD - MLA_QK))).reshape(MLA_Q_RANK, MLA_HEADS * MLA_QK_PAD)
    ukv = w_ukv[l].reshape(MLA_KV_RANK, MLA_HEADS, MLA_NOPE + MLA_V)
    ukv = jnp.concatenate([ukv[:, :, :MLA_NOPE].reshape(MLA_KV_RANK, -1),
                           ukv[:, :, MLA_NOPE:].reshape(MLA_KV_RANK, -1)], axis=1)
    head_id = jnp.arange(LANES) // HEAD_DIM
    bd = jnp.where(head_id[:, None] == head_id[None, :], 1.0 / HEAD_DIM, 0.0).astype(BF16)
    row = lambda v: v.reshape(1, -1).astype(F32)
    return dict(
        w_s=w_s, w_g=wg.astype(BF16), w_uq=uq.astype(BF16), w_ukv=ukv.astype(BF16),
        g_q=row(mla_q_norm[l]), g_kv=row(mla_kv_norm[l]),
        g_aq=row(jnp.tile(ax_q_norm[l], LANES // HEAD_DIM)), g_ak=row(jnp.tile(ax_k_norm[l], LANES // HEAD_DIM)),
        bd=bd,
        w_br_mla=w_br_mla[l].astype(BF16),
        w_br_win=_chunk_heads(w_br_win[l], 0).astype(BF16),
        w_br_ax=_chunk_heads(w_br_ax[l], 0).astype(BF16),
        w_out=w_out[l].astype(BF16),
        ln1_g=row(ln1_g[l]), ln1_b=row(ln1_b[l]), ln2_g=row(ln2_g[l]), ln2_b=row(ln2_b[l]),
    )


def kernel(x_prompt, x_sample, emb_ln_g, emb_ln_b, w_in, mla_q_norm, mla_kv_norm, w_uq, w_ukv, win_sink,
           ax_q_norm, ax_k_norm, w_br_mla, w_br_win, w_br_ax, w_out, ln1_g, ln1_b, ln2_g, ln2_b,
           ffn_w1, ffn_w3, ffn_w2, moe_router, moe_w1, moe_w3, moe_w2):
    depth = w_in.shape[0]
    alpha = (2 * depth) ** 0.25
    groups = []
    off = 0
    for xg in (x_prompt, x_sample):
        bsz, seq, _ = xg.shape
        assert off % seq == 0 and seq % GRID_W == 0
        groups.append((off, bsz, seq))
        off += bsz * seq
    n = off
    seq_min = min(g[2] for g in groups)
    seq_max = max(g[2] for g in groups)
    tm = _pick_tile(seq_min, TOKEN_TILE)
    tm_e = EXPERT_TILE
    x = jnp.concatenate([x_prompt.reshape(-1, D_MODEL), x_sample.reshape(-1, D_MODEL)], axis=0)

    tab = _rope_tables(seq_max)
    win_bias = _window_bias()
    pos_blk = jnp.concatenate([jnp.tile(jnp.arange(seq // tm, dtype=jnp.int32), bsz)
                               for (_, bsz, seq) in groups])

    x = _embed_ln(x, emb_ln_g, emb_ln_b, tm)
    o_a = o_b = o_c = None
    for l in range(depth):
        lw = _prep_layer(l, w_in, mla_q_norm, mla_kv_norm, w_uq, w_ukv, ax_q_norm, ax_k_norm,
                         w_br_mla, w_br_win, w_br_ax, w_out, ln1_g, ln1_b, ln2_g, ln2_b)
        qm, km, vm, wq, wk, wv, aq, ak, av = _in_proj(x, pos_blk, lw, tab, tm)
        sink = win_sink[l].astype(F32)
        for g in groups:
            o_a = _mla_attention(qm, km, vm, o_a, *g)
            o_b = _window_attention(wq, wk, wv, sink, win_bias, o_b, *g)
            o_c = _axial_attention(aq, ak, av, o_c, *g)
        x = _merge(x, o_a, o_b, o_c, lw, alpha, tm)
        i = l // 2
        if l % 2 == 0:
            d_ff = ffn_w1.shape[2]
            x = _dense_ffn(x, ffn_w1[i].astype(BF16), ffn_w3[i].astype(BF16), ffn_w2[i].astype(BF16),
                           lw["ln2_g"], lw["ln2_b"], alpha, tm, d_ff)
        else:
            router = jnp.pad(moe_router[i], ((0, 0), (0, LANES - N_EXPERTS)))
            r_hi = router.astype(BF16)
            r_lo = (router - r_hi.astype(F32)).astype(BF16)
            d_ffe = moe_w1.shape[3]
            e_out, gates = _router(x, r_hi, r_lo, tm)
            x = _moe_layer(x, e_out, gates, moe_w1[i].astype(BF16), moe_w3[i].astype(BF16),
                           moe_w2[i].astype(BF16), lw["ln2_g"], lw["ln2_b"], alpha, tm, tm_e,
                           d_ffe // 2)
    n_p = x_prompt.shape[0] * x_prompt.shape[1]
    return (x[:n_p].reshape(x_prompt.shape), x[n_p:].reshape(x_sample.shape))
```
